```python
import jax, jax.numpy as jnp
from jax import lax
import numpy as np

D_MODEL = 1024
BATCH = 2
SEQ = 16384
DEPTH = 4
DEC_BATCH = 8
DEC_SEQ = 4096
PAST_LEN = 128

N_EVEN = (DEPTH + 1) // 2
N_ODD = DEPTH // 2
D_FF = 2816
CHUNK = 64
LN_EPS = 1e-5
ALPHA = (2 * DEPTH) ** 0.25
BETA = (8 * DEPTH) ** -0.25
NEG_BIG = -1e30
TINY = 1e-30

A_HEADS = 4
A_DH = 128
A_W = A_HEADS * A_DH
CONV_W = 5
B_HEADS = 4
B_DK = 64
B_DV = 128
B_KW = B_HEADS * B_DK
B_VW = B_HEADS * B_DV
B_RANK = 16
GLA_GATE_NORMALIZER = 16.0
C_HEADS = 4
C_DK = 128
C_DV = 128
C_KW = C_HEADS * C_DK
C_VW = C_HEADS * C_DV
R_HEADS = 4
R_DK = 128
R_DV = 128
R_KW = R_HEADS * R_DK
R_VW = R_HEADS * R_DV
ROPE_BASE = 10000.0

EV_SIZES = (A_W, A_W, A_W, A_W, 4 * A_HEADS, B_KW, B_KW, B_VW, B_VW, 2 * B_RANK)
EV_COLS = 4 * A_W + 4 * A_HEADS + 2 * B_KW + 2 * B_VW + 2 * B_RANK
EV_MIX = A_W + B_VW
OD_SIZES = (C_KW, C_KW, C_KW, C_VW, C_VW, R_KW, R_KW, R_VW, R_VW)
OD_COLS = 3 * C_KW + 2 * C_VW + 2 * R_KW + 2 * R_VW
OD_MIX = C_VW + R_VW

kernel_name = "hybrid_bidir_mlstm_gla_hgrn2_retnet_encoder"


def _split(y, sizes):
    out, off = [], 0
    for s in sizes:
        out.append(y[..., off:off + s])
        off += s
    return out


def _heads(t, n_heads):
    b, n, c = t.shape
    return t.reshape(b, n, n_heads, c // n_heads).transpose(0, 2, 1, 3)


def _merge_heads(t):
    b, h, n, d = t.shape
    return t.transpose(0, 2, 1, 3).reshape(b, n, h * d)


def _dir_stack(fwd, bwd):
    return jnp.concatenate([fwd, jnp.flip(bwd, axis=2)], axis=0)


def _dir_merge(y):
    b = y.shape[0] // 2
    return y[:b] + jnp.flip(y[b:], axis=2)


def _to_chunks(t):
    b, h, n, d = t.shape
    return jnp.moveaxis(t.reshape(b, h, n // CHUNK, CHUNK, d), 2, 0)


def _from_chunks(t):
    nc, b, h, l, d = t.shape
    return jnp.moveaxis(t, 0, 2).reshape(b, h, nc * l, d)


def _layer_norm(x, g, b):
    xf = x.astype(jnp.float32)
    mu = jnp.mean(xf, -1, keepdims=True)
    var = jnp.mean(jnp.square(xf - mu), -1, keepdims=True)
    return ((xf - mu) * lax.rsqrt(var + LN_EPS) * g.astype(jnp.float32) + b.astype(jnp.float32)).astype(x.dtype)


def _head_layernorm(x):
    xf = x.astype(jnp.float32)
    mu = jnp.mean(xf, -1, keepdims=True)
    var = jnp.mean(jnp.square(xf - mu), -1, keepdims=True)
    return ((xf - mu) * lax.rsqrt(var + LN_EPS)).astype(x.dtype)


def _head_rmsnorm(x):
    xf = x.astype(jnp.float32)
    return (xf * lax.rsqrt(jnp.mean(jnp.square(xf), -1, keepdims=True) + LN_EPS)).astype(x.dtype)


def _swiglu(x, w_in, w_out):
    g, u = jnp.split(x @ w_in, 2, axis=-1)
    return (jax.nn.silu(g) * u) @ w_out


def _centred_dwconv(x, w):
    pad = CONV_W // 2
    n = x.shape[1]
    xp = jnp.pad(x, ((0, 0), (pad, pad), (0, 0)))
    acc = xp[:, 0:n, :] * w[0]
    for j in range(1, CONV_W):
        acc = acc + xp[:, j:j + n, :] * w[j]
    return acc


def _rotary(x):
    n, d = x.shape[2], x.shape[3]
    inv = 1.0 / (ROPE_BASE ** (jnp.arange(0, d, 2, dtype=jnp.float32) / d))
    ang = jnp.arange(n, dtype=jnp.float32)[:, None] * inv[None, :]
    cos, sin = jnp.cos(ang), jnp.sin(ang)
    xf = x.astype(jnp.float32)
    x1, x2 = xf[..., : d // 2], xf[..., d // 2:]
    return jnp.concatenate([x1 * cos - x2 * sin, x1 * sin + x2 * cos], -1).astype(x.dtype)


def _causal_mask():
    return jnp.tril(jnp.ones((CHUNK, CHUNK), dtype=bool))


def gated_linear_chunked(q, k, v, log_f):
    b, h, n, dk = q.shape
    dv = v.shape[-1]
    xs = tuple(_to_chunks(t.astype(jnp.float32)) for t in (q, k, v, log_f))
    mask = _causal_mask()
    vector_decay = log_f.shape[-1] > 1

    def step(state, inp):
        qc, kc, vc, gc = inp
        bcum = jnp.cumsum(gc, axis=-2)
        b_end = bcum[..., -1:, :]
        o = jnp.einsum('bhtd,bhde->bhte', qc * jnp.exp(bcum), state)
        if vector_decay:
            rel = jnp.where(mask[:, :, None], bcum[..., :, None, :] - bcum[..., None, :, :], NEG_BIG)
            scores = jnp.einsum('bhtd,bhsd,bhtsd->bhts', qc, kc, jnp.exp(rel))
        else:
            rel = jnp.where(mask, bcum[..., :, None, 0] - bcum[..., None, :, 0], NEG_BIG)
            scores = jnp.einsum('bhtd,bhsd->bhts', qc, kc) * jnp.exp(rel)
        o = o + jnp.einsum('bhts,bhse->bhte', scores, vc)
        state = jnp.exp(jnp.swapaxes(b_end, -1, -2)) * state + jnp.einsum('bhsd,bhse->bhde', kc * jnp.exp(b_end - bcum), vc)
        return state, o

    s0 = jnp.zeros((b, h, dk, dv), jnp.float32)
    _, o = lax.scan(step, s0, xs)
    return _from_chunks(o).astype(v.dtype)


def mlstm_chunked(q, k, v, log_i, log_f):
    b, h, n, dk = q.shape
    dv = v.shape[-1]
    xs = tuple(_to_chunks(t.astype(jnp.float32)) for t in (q, k, v, log_i, log_f))
    mask = _causal_mask()

    def step(carry, inp):
        c_mat, n_vec, m = carry
        qc, kc, vc, ic, fc = inp
        ic, fc = ic[..., 0], fc[..., 0]
        bcum = jnp.cumsum(fc, axis=-1)
        dmat = jnp.where(mask, bcum[..., :, None] - bcum[..., None, :] + ic[..., None, :], NEG_BIG)
        m_inter = bcum + m[..., None]
        m_t = jnp.maximum(m_inter, jnp.max(dmat, axis=-1))
        att = jnp.einsum('bhtd,bhsd->bhts', qc, kc) * jnp.exp(dmat - m_t[..., None])
        sc = jnp.exp(m_inter - m_t)
        num = sc[..., None] * jnp.einsum('bhtd,bhde->bhte', qc, c_mat) + jnp.einsum('bhts,bhse->bhte', att, vc)
        den = sc * jnp.einsum('bhtd,bhd->bht', qc, n_vec) + jnp.sum(att, -1)
        out = num / jnp.maximum(jnp.abs(den), jnp.exp(-m_t))[..., None]
        g_end = bcum[..., -1:] - bcum + ic
        m_new = jnp.maximum(bcum[..., -1] + m, jnp.max(g_end, -1))
        decay = jnp.exp(bcum[..., -1] + m - m_new)
        wk = kc * jnp.exp(g_end - m_new[..., None])[..., None]
        c_mat = decay[..., None, None] * c_mat + jnp.einsum('bhsd,bhse->bhde', wk, vc)
        n_vec = decay[..., None] * n_vec + jnp.sum(wk, -2)
        return (c_mat, n_vec, m_new), out

    init = (jnp.zeros((b, h, dk, dv), jnp.float32), jnp.zeros((b, h, dk), jnp.float32), jnp.zeros((b, h), jnp.float32))
    _, o = lax.scan(step, init, xs)
    return _from_chunks(o).astype(v.dtype)


def _even_mixer(x, w_in, gate_b, conv_w, a2_w, a2_b, norm_w, w_out):
    bz = x.shape[0]
    y = x @ w_in
    aq, ak, av, ao, ag, bq, bk, bv, bg, ba = _split(y, EV_SIZES)
    qk = jax.nn.silu(_centred_dwconv(jnp.concatenate([aq, ak], -1), conv_w))
    q = _heads(qk[..., :A_W], A_HEADS)
    k = _heads(qk[..., A_W:], A_HEADS) * A_DH ** -0.5
    v = _heads(av, A_HEADS)
    gates = (ag + gate_b).astype(jnp.float32)
    gates = gates.reshape(bz, -1, 4, A_HEADS).transpose(2, 0, 3, 1)[..., None]
    log_i = _dir_stack(gates[0], gates[2])
    log_f = jax.nn.log_sigmoid(_dir_stack(gates[1], gates[3]))
    h_a = _dir_merge(mlstm_chunked(_dir_stack(q, q), _dir_stack(k, k), _dir_stack(v, v), log_i, log_f))
    h_a = _merge_heads(_head_layernorm(h_a)) * norm_w[:A_W] * jax.nn.sigmoid(ao)
    q = _heads(bq, B_HEADS) * B_DK ** -0.5
    k = _heads(bk, B_HEADS)
    v = _heads(bv, B_HEADS)
    a_pre = jnp.einsum('bnjr,jrc->jbnc', ba.reshape(bz, -1, 2, B_RANK), a2_w) + a2_b[:, None, None, :]
    log_a = jax.nn.log_sigmoid(a_pre.astype(jnp.float32)) / GLA_GATE_NORMALIZER
    log_a = _dir_stack(_heads(log_a[0], B_HEADS), _heads(log_a[1], B_HEADS))
    h_b = _dir_merge(gated_linear_chunked(_dir_stack(q, q), _dir_stack(k, k), _dir_stack(v, v), log_a))
    h_b = _merge_heads(_head_rmsnorm(h_b)) * norm_w[A_W:] * jax.nn.silu(bg)
    return jnp.concatenate([h_a, h_b], -1) @ w_out


def _odd_mixer(x, w_in, lb_logits, layer_idx, norm_w, w_out):
    bz = x.shape[0]
    y = x @ w_in
    cq, cf_fwd, cf_bwd, ci, cg, rq, rk, rv, rg = _split(y, OD_SIZES)
    p = jax.nn.softmax(lb_logits.astype(jnp.float32), axis=0)
    lb = (jnp.cumsum(p, axis=0) - p[0])[layer_idx]

    def forget(z, lb_dir):
        z = z.astype(jnp.float32)
        f = lb_dir + (1.0 - lb_dir) * jax.nn.sigmoid(z)
        log_f = jnp.log(jnp.maximum(f, TINY))
        key = (1.0 - lb_dir) * jax.nn.sigmoid(-z)
        return _heads(log_f, C_HEADS), _heads(key, C_HEADS)

    lf_f, k_f = forget(cf_fwd, lb[0])
    lf_b, k_b = forget(cf_bwd, lb[1])
    q = _heads(jax.nn.silu(cq), C_HEADS) * C_DK ** -0.5
    v = _heads(ci, C_HEADS)
    h_c = _dir_merge(gated_linear_chunked(_dir_stack(q, q), _dir_stack(k_f, k_b), _dir_stack(v, v), _dir_stack(lf_f, lf_b)))
    h_c = _merge_heads(_head_rmsnorm(h_c)) * norm_w * jax.nn.silu(cg)
    q = _rotary(_heads(rq, R_HEADS))
    k = _rotary(_heads(rk, R_HEADS)) * R_DK ** -0.5
    v = _heads(rv, R_HEADS)
    n = x.shape[1]
    log_gamma = jnp.log1p(-(2.0 ** (-5.0 - jnp.arange(R_HEADS, dtype=jnp.float32))))
    lg = jnp.broadcast_to(log_gamma[None, :, None, None], (2 * bz, R_HEADS, n, 1))
    h_d = _dir_merge(gated_linear_chunked(_dir_stack(q, q), _dir_stack(k, k), _dir_stack(v, v), lg))
    h_d = _merge_heads(_head_layernorm(h_d)) * jax.nn.silu(rg)
    return jnp.concatenate([h_c, h_d], -1) @ w_out


def _trunk(x, ffn1_w_in, ffn1_w_out, ffn2_w_in, ffn2_w_out, ln_g, ln_b,
           ev_w_in, ev_gate_b, ev_conv_w, ev_gla_a2_w, ev_gla_a2_b, ev_norm_w, ev_w_out,
           od_w_in, od_lb_logits, od_norm_w, od_w_out):
    for l in range(DEPTH):
        j = l // 2
        x = _layer_norm(ALPHA * x + 0.5 * _swiglu(x, ffn1_w_in[l], ffn1_w_out[l]), ln_g[l, 0], ln_b[l, 0])
        if l % 2 == 0:
            mix = _even_mixer(x, ev_w_in[j], ev_gate_b[j], ev_conv_w[j], ev_gla_a2_w[j], ev_gla_a2_b[j], ev_norm_w[j], ev_w_out[j])
        else:
            mix = _odd_mixer(x, od_w_in[j], od_lb_logits, j, od_norm_w[j], od_w_out[j])
        x = _layer_norm(ALPHA * x + mix, ln_g[l, 1], ln_b[l, 1])
        x = _layer_norm(ALPHA * x + 0.5 * _swiglu(x, ffn2_w_in[l], ffn2_w_out[l]), ln_g[l, 2], ln_b[l, 2])
    return x


def setup_inputs(seed: int = 0) -> dict:
    key = jax.random.key(seed)
    ks = jax.random.split(key, 20)

    def nrm(k, shape, scale):
        return jax.random.normal(k, shape, jnp.float32) * scale

    f_bias = jnp.linspace(3.0, 6.0, A_HEADS, dtype=jnp.float32)
    i_bias = jnp.zeros((A_HEADS,), jnp.float32)
    gate_base = jnp.concatenate([i_bias, f_bias, i_bias, f_bias])
    return {
        "x_prompt": nrm(ks[0], (BATCH, SEQ, D_MODEL), 1.0),
        "x_sample": nrm(ks[1], (DEC_BATCH, DEC_SEQ, D_MODEL), 1.0),
        "ffn1_w_in": nrm(ks[2], (DEPTH, D_MODEL, 2 * D_FF), D_MODEL ** -0.5),
        "ffn1_w_out": nrm(ks[3], (DEPTH, D_FF, D_MODEL), BETA * D_FF ** -0.5),
        "ffn2_w_in": nrm(ks[4], (DEPTH, D_MODEL, 2 * D_FF), D_MODEL ** -0.5),
        "ffn2_w_out": nrm(ks[5], (DEPTH, D_FF, D_MODEL), BETA * D_FF ** -0.5),
        "ln_g": 1.0 + nrm(ks[6], (DEPTH, 3, D_MODEL), 0.02),
        "ln_b": nrm(ks[7], (DEPTH, 3, D_MODEL), 0.02),
        "ev_w_in": nrm(ks[8], (N_EVEN, D_MODEL, EV_COLS), D_MODEL ** -0.5),
        "ev_gate_b": gate_base[None, :] + nrm(ks[9], (N_EVEN, 4 * A_HEADS), 0.1),
        "ev_conv_w": nrm(ks[10], (N_EVEN, CONV_W, 2 * A_W), CONV_W ** -0.5),
        "ev_gla_a2_w": nrm(ks[11], (N_EVEN, 2, B_RANK, B_KW), B_RANK ** -0.5),
        "ev_gla_a2_b": nrm(ks[12], (N_EVEN, 2, B_KW), 0.02),
        "ev_norm_w": 1.0 + nrm(ks[13], (N_EVEN, EV_MIX), 0.02),
        "ev_w_out": nrm(ks[14], (N_EVEN, EV_MIX, D_MODEL), BETA * EV_MIX ** -0.5),
        "od_w_in": nrm(ks[15], (N_ODD, D_MODEL, OD_COLS), D_MODEL ** -0.5),
        "od_lb_logits": nrm(ks[16], (N_ODD, 2, C_KW), 0.5),
        "od_norm_w": 1.0 + nrm(ks[17], (N_ODD, C_VW), 0.02),
        "od_w_out": nrm(ks[18], (N_ODD, OD_MIX, D_MODEL), BETA * OD_MIX ** -0.5),
    }


def reference(x_prompt, x_sample, ffn1_w_in, ffn1_w_out, ffn2_w_in, ffn2_w_out, ln_g, ln_b,
              ev_w_in, ev_gate_b, ev_conv_w, ev_gla_a2_w, ev_gla_a2_b, ev_norm_w, ev_w_out,
              od_w_in, od_lb_logits, od_norm_w, od_w_out):
    y_prompt = _trunk(x_prompt, ffn1_w_in, ffn1_w_out, ffn2_w_in, ffn2_w_out, ln_g, ln_b,
                      ev_w_in, ev_gate_b, ev_conv_w, ev_gla_a2_w, ev_gla_a2_b, ev_norm_w, ev_w_out,
                      od_w_in, od_lb_logits, od_norm_w, od_w_out)
    y_sample = _trunk(x_sample, ffn1_w_in, ffn1_w_out, ffn2_w_in, ffn2_w_out, ln_g, ln_b,
                      ev_w_in, ev_gate_b, ev_conv_w, ev_gla_a2_w, ev_gla_a2_b, ev_norm_w, ev_w_out,
                      od_w_in, od_lb_logits, od_norm_w, od_w_out)
    return (y_prompt, y_sample)
```

```python
import functools
import math

import numpy as np
import jax
import jax.numpy as jnp
from jax import lax
from jax.experimental import pallas as pl
from jax.experimental.pallas import tpu as pltpu

F32 = jnp.float32
BF16 = jnp.bfloat16

D_MODEL = 1024
DEPTH = 4
D_FF = 2816
LN_EPS = 1e-5
ALPHA = (2 * DEPTH) ** 0.25
NEG_BIG = -1e30
TINY = 1e-30

HEADS = 4
DH = 128
MIX_W = HEADS * DH
A_W = MIX_W
CONV_W = 5
B_DK = 64
B_KW = HEADS * B_DK
B_RANK = 16
GLA_GATE_NORMALIZER = 16.0
ROPE_BASE = 10000.0

LANES = 128
SUBLANES = 8
VMEM_LIMIT = 56 * 1024 * 1024

FFN_TM = 512
FFN_FC = 256
PROJ_TM = 512
SCALAR_CHUNK = 256
VEC_CHUNK = 128
VEC_SUB = 8


def _dot(a, b):
    return jnp.dot(a, b, preferred_element_type=F32)


def _dot_nt(a, b):
    return lax.dot_general(a, b, (((1,), (1,)), ((), ())), preferred_element_type=F32)


def _dot_tn(a, b):
    return lax.dot_general(a, b, (((0,), (0,)), ((), ())), preferred_element_type=F32)


def _dot_exact(w_bf16, x):
    hi = x.astype(BF16)
    r1 = x - hi.astype(F32)
    mid = r1.astype(BF16)
    lo = (r1 - mid.astype(F32)).astype(BF16)
    return _dot(w_bf16, hi) + _dot(w_bf16, mid) + _dot(w_bf16, lo)


def _sigmoid(x):
    return 1.0 / (1.0 + jnp.exp(-x))


def _silu(x):
    return x * _sigmoid(x)


def _log_sigmoid(x):
    return jnp.minimum(x, 0.0) - jnp.log1p(jnp.exp(-jnp.abs(x)))


def _layer_norm(y, g, b):
    mu = jnp.mean(y, -1, keepdims=True)
    d = y - mu
    var = jnp.mean(d * d, -1, keepdims=True)
    return d * lax.rsqrt(var + LN_EPS) * g + b


def _head_ln(x):
    mu = jnp.mean(x, -1, keepdims=True)
    d = x - mu
    var = jnp.mean(d * d, -1, keepdims=True)
    return d * lax.rsqrt(var + LN_EPS)


def _head_rms(x):
    return x * lax.rsqrt(jnp.mean(x * x, -1, keepdims=True) + LN_EPS)


def _params(n_grid):
    return pltpu.CompilerParams(dimension_semantics=("arbitrary",) * n_grid,
                                vmem_limit_bytes=VMEM_LIMIT)


def _full_spec(shape):
    nd = len(shape)
    return pl.BlockSpec(shape, lambda *_: (0,) * nd)


def _ffn_kernel(x_ref, win_ref, wout_ref, g_ref, b_ref, o_ref):
    x = x_ref[...]
    xb = x.astype(BF16)
    acc = jnp.zeros(x.shape, F32)
    for j in range(D_FF // FFN_FC):
        lo = j * FFN_FC
        g = _dot(xb, win_ref[:, lo:lo + FFN_FC])
        u = _dot(xb, win_ref[:, D_FF + lo:D_FF + lo + FFN_FC])
        h = (_silu(g) * u).astype(BF16)
        acc = acc + _dot(h, wout_ref[lo:lo + FFN_FC, :])
    o_ref[...] = _layer_norm(ALPHA * x + 0.5 * acc, g_ref[...], b_ref[...])


def _ffn_ln(x2d, w_in, w_out, g, b):
    t = x2d.shape[0]
    tm = min(FFN_TM, t)
    return pl.pallas_call(
        _ffn_kernel,
        grid=(t // tm,),
        in_specs=[pl.BlockSpec((tm, D_MODEL), lambda i: (i, 0)),
                  _full_spec(w_in.shape), _full_spec(w_out.shape),
                  _full_spec(g.shape), _full_spec(b.shape)],
        out_specs=pl.BlockSpec((tm, D_MODEL), lambda i: (i, 0)),
        out_shape=jax.ShapeDtypeStruct((t, D_MODEL), F32),
        compiler_params=_params(1),
        name="ffn_ln",
    )(x2d, w_in, w_out, g, b)


EV_REST = 3 * MIX_W + 2 * B_KW + 2 * MIX_W - MIX_W + 2 * LANES
_EV_OFF = {"av": 0, "ao": 512, "bq": 1024, "bk": 1280, "bv": 1536, "bg": 2048, "ag": 2560, "ba": 2688}
HALO = SUBLANES


def _even_in_kernel(xp_ref, xc_ref, xn_ref, wqk_ref, wrest_ref, convw_ref, gateb_ref, a2w_ref, a2b_ref,
                    qa_ref, ka_ref, va_ref, oga_ref, gates_ref, qb_ref, kb_ref, vb_ref, gb_ref, loga_ref,
                    ybuf):
    i = pl.program_id(1)
    last = pl.num_programs(1) - 1
    tm = xc_ref.shape[1]
    xc = xc_ref[0].astype(BF16)
    xp = jnp.where(i > 0, xp_ref[0], 0.0).astype(BF16)
    xn = jnp.where(i < last, xn_ref[0], 0.0).astype(BF16)
    wqk = wqk_ref[...]
    ybuf[0:HALO, :] = _dot(xp, wqk)
    ybuf[HALO:HALO + tm, :] = _dot(xc, wqk)
    ybuf[HALO + tm:2 * HALO + tm, :] = _dot(xn, wqk)
    base = HALO - CONV_W // 2
    acc = ybuf[pl.ds(base, tm), :] * convw_ref[0:1, :]
    for j in range(1, CONV_W):
        acc = acc + ybuf[pl.ds(base + j, tm), :] * convw_ref[j:j + 1, :]
    qk = _silu(acc)
    qa_ref[0] = qk[:, :A_W].astype(BF16)
    ka_ref[0] = (qk[:, A_W:] * DH ** -0.5).astype(BF16)

    def sect(name, width):
        lo = _EV_OFF[name]
        return _dot(xc, wrest_ref[:, lo:lo + width])

    va_ref[0] = sect("av", MIX_W).astype(BF16)
    oga_ref[0] = _sigmoid(sect("ao", MIX_W)).astype(BF16)
    qb_ref[0] = (sect("bq", B_KW) * B_DK ** -0.5).astype(BF16)
    kb_ref[0] = sect("bk", B_KW).astype(BF16)
    vb_ref[0] = sect("bv", MIX_W).astype(BF16)
    gb_ref[0] = _silu(sect("bg", MIX_W)).astype(BF16)
    gates = sect("ag", LANES) + gateb_ref[...]
    col = lax.broadcasted_iota(jnp.int32, gates.shape, 1)
    is_f = (col % 8) >= 4
    gates_ref[0] = jnp.where(is_f, _log_sigmoid(gates), gates)
    ba = sect("ba", LANES).astype(BF16)
    a_pre = _dot(ba, a2w_ref[...]) + a2b_ref[...]
    loga_ref[0] = _log_sigmoid(a_pre) * (1.0 / GLA_GATE_NORMALIZER)


def _even_in(x, wqk, wrest, convw, gateb, a2w, a2b):
    bsz, n, _ = x.shape
    tm = min(PROJ_TM, n)
    nt = n // tm
    r = tm // HALO
    nblk8 = n // HALO

    def tile(w, dt):
        return (pl.BlockSpec((1, tm, w), lambda b, i: (b, i, 0)), jax.ShapeDtypeStruct((bsz, n, w), dt))

    outs = [tile(A_W, BF16), tile(A_W, BF16), tile(MIX_W, BF16), tile(MIX_W, BF16), tile(LANES, F32),
            tile(B_KW, BF16), tile(B_KW, BF16), tile(MIX_W, BF16), tile(MIX_W, BF16), tile(2 * B_KW, F32)]
    return pl.pallas_call(
        _even_in_kernel,
        grid=(bsz, nt),
        in_specs=[pl.BlockSpec((1, HALO, D_MODEL), lambda b, i: (b, jnp.maximum(i * r - 1, 0), 0)),
                  pl.BlockSpec((1, tm, D_MODEL), lambda b, i: (b, i, 0)),
                  pl.BlockSpec((1, HALO, D_MODEL), lambda b, i: (b, jnp.minimum((i + 1) * r, nblk8 - 1), 0)),
                  _full_spec(wqk.shape), _full_spec(wrest.shape), _full_spec(convw.shape),
                  _full_spec(gateb.shape), _full_spec(a2w.shape), _full_spec(a2b.shape)],
        out_specs=[o[0] for o in outs],
        out_shape=[o[1] for o in outs],
        scratch_shapes=[pltpu.VMEM((tm + 2 * HALO, 2 * A_W), F32)],
        compiler_params=_params(2),
        name="even_in",
    )(x, x, x, wqk, wrest, convw, gateb, a2w, a2b)


def _odd_in_kernel(x_ref, w_ref, lb_ref, cos_ref, sin_ref,
                   qc_ref, lff_ref, lfb_ref, kcf_ref, kcb_ref, vc_ref, gc_ref,
                   qr_ref, kr_ref, vr_ref, gr_ref):
    xb = x_ref[0].astype(BF16)

    def sect(idx):
        return _dot(xb, w_ref[:, idx * MIX_W:(idx + 1) * MIX_W])

    qc_ref[0] = (_silu(sect(0)) * DH ** -0.5).astype(BF16)
    for d, (lf_ref, k_ref) in enumerate(((lff_ref, kcf_ref), (lfb_ref, kcb_ref))):
        z = sect(1 + d)
        lb = lb_ref[d:d + 1, :]
        f = lb + (1.0 - lb) * _sigmoid(z)
        lf_ref[0] = jnp.log(jnp.maximum(f, TINY))
        k_ref[0] = ((1.0 - lb) * _sigmoid(-z)).astype(BF16)
    vc_ref[0] = sect(3).astype(BF16)
    gc_ref[0] = _silu(sect(4)).astype(BF16)
    cos2 = cos_ref[...]
    sin2 = sin_ref[...]

    def rotary(y, scale):
        parts = []
        for h in range(HEADS):
            yh = y[:, h * DH:(h + 1) * DH]
            parts.append((yh * cos2 + pltpu.roll(yh, DH // 2, axis=1) * sin2) * scale)
        return jnp.concatenate(parts, axis=-1).astype(BF16)

    qr_ref[0] = rotary(sect(5), 1.0)
    kr_ref[0] = rotary(sect(6), DH ** -0.5)
    vr_ref[0] = sect(7).astype(BF16)
    gr_ref[0] = _silu(sect(8)).astype(BF16)


def _odd_in(x, w, lb, cos2, sin2):
    bsz, n, _ = x.shape
    tm = min(PROJ_TM, n)

    def tile(dt):
        return (pl.BlockSpec((1, tm, MIX_W), lambda b, i: (b, i, 0)), jax.ShapeDtypeStruct((bsz, n, MIX_W), dt))

    dts = [BF16, F32, F32, BF16, BF16, BF16, BF16, BF16, BF16, BF16, BF16]
    outs = [tile(dt) for dt in dts]
    return pl.pallas_call(
        _odd_in_kernel,
        grid=(bsz, n // tm),
        in_specs=[pl.BlockSpec((1, tm, D_MODEL), lambda b, i: (b, i, 0)),
                  _full_spec(w.shape), _full_spec(lb.shape),
                  pl.BlockSpec((tm, DH), lambda b, i: (i, 0)),
                  pl.BlockSpec((tm, DH), lambda b, i: (i, 0))],
        out_specs=[o[0] for o in outs],
        out_shape=[o[1] for o in outs],
        compiler_params=_params(2),
        name="odd_in",
    )(x, w, lb, cos2, sin2)


def _hgrn_lower_bound(lb_logits, layer_idx):
    n_odd = lb_logits.shape[0]

    def kern(l_ref, o_ref):
        rows = [l_ref[i] for i in range(n_odd)]
        mx = rows[0]
        for r in rows[1:]:
            mx = jnp.maximum(mx, r)
        es = [jnp.exp(r - mx) for r in rows]
        tot = es[0]
        for e in es[1:]:
            tot = tot + e
        acc = jnp.zeros_like(tot)
        for i in range(1, layer_idx + 1):
            acc = acc + es[i] / tot
        o_ref[...] = acc

    return pl.pallas_call(
        kern,
        out_shape=jax.ShapeDtypeStruct(lb_logits.shape[1:], F32),
        name="hgrn_lb",
    )(lb_logits)


def _mlstm_kernel(cum_ref, qf_ref, kf_ref, vf_ref, gf_ref, qb_ref, kb_ref, vb_ref, gb_ref,
                  of_ref, ob_ref, c_ref, n_ref, m_ref):
    c = qf_ref.shape[1]

    @pl.when(pl.program_id(1) == 0)
    def _():
        c_ref[...] = jnp.zeros_like(c_ref)
        n_ref[...] = jnp.zeros_like(n_ref)
        m_ref[...] = jnp.zeros_like(m_ref)

    row = lax.broadcasted_iota(jnp.int32, (c, c), 0)
    colm = lax.broadcasted_iota(jnp.int32, (c, c), 1)
    dirs = ((qf_ref, kf_ref, vf_ref, gf_ref, of_ref, colm <= row, c - 1),
            (qb_ref, kb_ref, vb_ref, gb_ref, ob_ref, colm >= row, 0))
    for d, (q_ref, k_ref, v_ref, g_ref, o_ref, mask, end) in enumerate(dirs):
        gates = g_ref[0]
        cum = _dot_exact(cum_ref[d], gates)
        gates_t = gates.T
        cum_t = cum.T
        for h in range(HEADS):
            ci, cf = 8 * d + h, 8 * d + 4 + h
            sl = slice(h * DH, (h + 1) * DH)
            q = q_ref[0, :, sl]
            k = k_ref[0, :, sl]
            v = v_ref[0, :, sl]
            i_col = gates[:, ci:ci + 1]
            b_col = cum[:, cf:cf + 1]
            i_row = gates_t[ci:ci + 1, :]
            b_row = cum_t[cf:cf + 1, :]
            tot = b_col[end:end + 1, :]
            s = d * HEADS + h
            m_old = m_ref[s:s + 1, 0:1]
            n_old = n_ref[s:s + 1, :]
            c_old = c_ref[s]
            dmat = jnp.where(mask, b_col - b_row + i_row, NEG_BIG)
            m_inter = b_col + m_old
            m_t = jnp.maximum(m_inter, jnp.max(dmat, axis=-1, keepdims=True))
            att = _dot_nt(q, k) * jnp.exp(dmat - m_t)
            sc = jnp.exp(m_inter - m_t)
            num = sc * _dot(q, c_old.astype(BF16)) + _dot(att.astype(BF16), v)
            qf32 = q.astype(F32)
            den = sc * jnp.sum(qf32 * n_old, axis=-1, keepdims=True) + jnp.sum(att, axis=-1, keepdims=True)
            o_ref[0, :, sl] = num / jnp.maximum(jnp.abs(den), jnp.exp(-m_t))
            g_end = tot - b_col + i_col
            m_new = jnp.maximum(tot + m_old, jnp.max(g_end, axis=0, keepdims=True))
            decay = jnp.exp(tot + m_old - m_new)
            wk = k.astype(F32) * jnp.exp(g_end - m_new)
            c_ref[s] = decay * c_old + _dot_tn(wk.astype(BF16), v)
            n_ref[s:s + 1, :] = decay * n_old + jnp.sum(wk, axis=0, keepdims=True)
            m_ref[s:s + 1, :] = jnp.broadcast_to(m_new, (1, LANES))


def _cum_mats(c):
    low = np.tril(np.ones((c, c), np.float32))
    return jnp.asarray(np.stack([low, low.T]), BF16)


def _mlstm(q, k, v, gates):
    bsz, n, _ = q.shape
    c = min(SCALAR_CHUNK, n)
    nc = n // c

    def fwd(w):
        return pl.BlockSpec((1, c, w), lambda b, i: (b, i, 0))

    def bwd(w):
        return pl.BlockSpec((1, c, w), lambda b, i: (b, nc - 1 - i, 0))

    cum = _cum_mats(c)
    out = jax.ShapeDtypeStruct((bsz, n, MIX_W), F32)
    return pl.pallas_call(
        _mlstm_kernel,
        grid=(bsz, nc),
        in_specs=[_full_spec(cum.shape),
                  fwd(MIX_W), fwd(MIX_W), fwd(MIX_W), fwd(LANES),
                  bwd(MIX_W), bwd(MIX_W), bwd(MIX_W), bwd(LANES)],
        out_specs=[fwd(MIX_W), bwd(MIX_W)],
        out_shape=[out, out],
        scratch_shapes=[pltpu.VMEM((2 * HEADS, DH, DH), F32),
                        pltpu.VMEM((2 * HEADS, DH), F32),
                        pltpu.VMEM((2 * HEADS, LANES), F32)],
        compiler_params=_params(2),
        name="mlstm_scan",
    )(cum, q, k, v, gates, q, k, v, gates)


def _vec_levels(c):
    lv, h = [], VEC_SUB
    while h < c:
        lv.append(h)
        h *= 2
    return lv


def _vec_consts(c):
    levels = _vec_levels(c)
    t = np.arange(c)
    mats = [np.tril(np.ones((c, c), np.float32))]
    lev = np.full((c, c), -1, np.int32)
    for li, h in enumerate(levels):
        w = np.zeros((c, c), np.float32)
        blk = t // (2 * h)
        mid = blk * 2 * h + h - 1
        second = (t // h) % 2 == 1
        for r in range(c):
            if second[r]:
                w[r, mid[r] + 1:r + 1] = 1.0
            else:
                w[r, r + 1:mid[r] + 1] = 1.0
        mats.append(w)
        pair = (blk[:, None] == blk[None, :]) & second[:, None] & (~second)[None, :]
        lev[pair] = li
    fwd = np.concatenate(mats, axis=0)
    bwd = np.concatenate([m[::-1, ::-1] for m in mats], axis=0)
    wall = jnp.asarray(np.stack([fwd, bwd]), BF16)
    levid = jnp.asarray(np.stack([lev, lev[::-1, ::-1]]), jnp.int32)
    return wall, levid, len(levels)


def _vec_kernel(dk, nlev, wall_ref, lev_ref, q_f, k_f, v_f, g_f, q_b, k_b, v_b, g_b,
                of_ref, ob_ref, st_ref):
    c = q_f.shape[1]
    sub = VEC_SUB

    @pl.when(pl.program_id(1) == 0)
    def _():
        st_ref[...] = jnp.zeros_like(st_ref)

    sub_iota = lax.broadcasted_iota(jnp.int32, (sub, 1), 0)
    lane_iota = lax.broadcasted_iota(jnp.int32, (sub, c), 1)
    dirs = ((q_f, k_f, v_f, g_f, of_ref, c - 1), (q_b, k_b, v_b, g_b, ob_ref, 0))
    for d, (q_ref, k_ref, v_ref, g_ref, o_ref, end) in enumerate(dirs):
        x_all = _dot_exact(wall_ref[d], g_ref[0])
        lev = lev_ref[d]
        for h in range(HEADS):
            ks = slice(h * dk, (h + 1) * dk)
            vs = slice(h * DH, (h + 1) * DH)
            s = d * HEADS + h
            q = q_ref[0, :, ks].astype(F32)
            k = k_ref[0, :, ks].astype(F32)
            v = v_ref[0, :, vs]
            b = x_all[0:c, ks]
            st = st_ref[s]
            o = _dot_nt((q * jnp.exp(b)).astype(BF16), st.astype(BF16))
            tiles = []
            for blk in range(c // sub):
                r0 = blk * sub
                bi, qi, ki = b[r0:r0 + sub], q[r0:r0 + sub], k[r0:r0 + sub]
                tile = jnp.zeros((sub, c), F32)
                for j in range(sub):
                    causal = (sub_iota >= j) if d == 0 else (sub_iota <= j)
                    diff = jnp.where(causal, bi - bi[j:j + 1, :], NEG_BIG)
                    colsum = jnp.sum(qi * (ki[j:j + 1, :] * jnp.exp(diff)), axis=-1, keepdims=True)
                    tile = jnp.where(lane_iota == r0 + j, colsum, tile)
                tiles.append(tile)
            a = jnp.concatenate(tiles, axis=0)
            for li in range(nlev):
                e = jnp.exp(x_all[(1 + li) * c:(2 + li) * c, ks])
                sc = _dot_nt((q * e).astype(BF16), (k * e).astype(BF16))
                a = jnp.where(lev == li, sc, a)
            o_ref[0, :, vs] = o + _dot(a.astype(BF16), v)
            b_end = b[end:end + 1, :]
            kw = (k * jnp.exp(b_end - b)).astype(BF16)
            st_ref[s] = st * jnp.exp(b_end) + _dot_tn(v, kw)


def _vec_scan(q, k_f, k_b, v, g_f, g_b, dk, name):
    bsz, n, _ = q.shape
    c = min(VEC_CHUNK, n)
    nc = n // c
    kw = HEADS * dk
    wall, levid, nlev = _vec_consts(c)

    def fwd(w):
        return pl.BlockSpec((1, c, w), lambda b, i: (b, i, 0))

    def bwd(w):
        return pl.BlockSpec((1, c, w), lambda b, i: (b, nc - 1 - i, 0))

    out = jax.ShapeDtypeStruct((bsz, n, MIX_W), F32)
    return pl.pallas_call(
        functools.partial(_vec_kernel, dk, nlev),
        grid=(bsz, nc),
        in_specs=[_full_spec(wall.shape), _full_spec(levid.shape),
                  fwd(kw), fwd(kw), fwd(MIX_W), fwd(kw),
                  bwd(kw), bwd(kw), bwd(MIX_W), bwd(kw)],
        out_specs=[fwd(MIX_W), bwd(MIX_W)],
        out_shape=[out, out],
        scratch_shapes=[pltpu.VMEM((2 * HEADS, DH, dk), F32)],
        compiler_params=_params(2),
        name=name,
    )(wall, levid, q, k_f, v, g_f, q, k_b, v, g_b)


def _ret_consts(c):
    log_gamma = np.log1p(-(2.0 ** (-5.0 - np.arange(HEADS, dtype=np.float64))))
    t = np.arange(c, dtype=np.float64)
    diff = t[:, None] - t[None, :]
    dmat = np.zeros((2, HEADS, c, c))
    qdec = np.zeros((2, c, MIX_W))
    kdec = np.zeros((2, c, MIX_W))
    for h in range(HEADS):
        lg = log_gamma[h]
        dmat[0, h] = np.where(diff >= 0, np.exp(lg * np.maximum(diff, 0)), 0.0)
        dmat[1, h] = dmat[0, h].T
        sl = slice(h * DH, (h + 1) * DH)
        qdec[0, :, sl] = np.exp(lg * (t + 1))[:, None]
        qdec[1, :, sl] = np.exp(lg * (c - t))[:, None]
        kdec[0, :, sl] = np.exp(lg * (c - 1 - t))[:, None]
        kdec[1, :, sl] = np.exp(lg * t)[:, None]
    sdec = [float(np.exp(lg * c)) for lg in log_gamma]
    return (jnp.asarray(dmat, F32), jnp.asarray(qdec, F32), jnp.asarray(kdec, F32)), sdec


def _ret_kernel(sdec, dmat_ref, qdec_ref, kdec_ref, q_f, k_f, v_f, q_b, k_b, v_b, of_ref, ob_ref, st_ref):
    @pl.when(pl.program_id(1) == 0)
    def _():
        st_ref[...] = jnp.zeros_like(st_ref)

    dirs = ((q_f, k_f, v_f, of_ref), (q_b, k_b, v_b, ob_ref))
    for d, (q_ref, k_ref, v_ref, o_ref) in enumerate(dirs):
        for h in range(HEADS):
            sl = slice(h * DH, (h + 1) * DH)
            s = d * HEADS + h
            q = q_ref[0, :, sl]
            k = k_ref[0, :, sl]
            v = v_ref[0, :, sl]
            st = st_ref[s]
            att = _dot_nt(q, k) * dmat_ref[d, h]
            qd = (q.astype(F32) * qdec_ref[d, :, sl]).astype(BF16)
            o_ref[0, :, sl] = _dot(qd, st.astype(BF16)) + _dot(att.astype(BF16), v)
            kd = (k.astype(F32) * kdec_ref[d, :, sl]).astype(BF16)
            st_ref[s] = sdec[h] * st + _dot_tn(kd, v)


def _retention(q, k, v):
    bsz, n, _ = q.shape
    c = min(SCALAR_CHUNK, n)
    nc = n // c
    consts, sdec = _ret_consts(c)

    def fwd(w):
        return pl.BlockSpec((1, c, w), lambda b, i: (b, i, 0))

    def bwd(w):
        return pl.BlockSpec((1, c, w), lambda b, i: (b, nc - 1 - i, 0))

    out = jax.ShapeDtypeStruct((bsz, n, MIX_W), F32)
    return pl.pallas_call(
        functools.partial(_ret_kernel, sdec),
        grid=(bsz, nc),
        in_specs=[_full_spec(a.shape) for a in consts] + [fwd(MIX_W)] * 3 + [bwd(MIX_W)] * 3,
        out_specs=[fwd(MIX_W), bwd(MIX_W)],
        out_shape=[out, out],
        scratch_shapes=[pltpu.VMEM((2 * HEADS, DH, DH), F32)],
        compiler_params=_params(2),
        name="retention_scan",
    )(*consts, q, k, v, q, k, v)


def _mix_out_kernel(norms, h1f_ref, h1b_ref, g1_ref, h2f_ref, h2b_ref, g2_ref, x_ref,
                    nw_ref, wout_ref, lg_ref, lb_ref, o_ref):
    parts = []
    groups = ((h1f_ref, h1b_ref, g1_ref, norms[0], 0), (h2f_ref, h2b_ref, g2_ref, norms[1], MIX_W))
    for hf_ref, hb_ref, gate_ref, norm, off in groups:
        hsum = hf_ref[...] + hb_ref[...]
        gate = gate_ref[...].astype(F32)
        for h in range(HEADS):
            sl = slice(h * DH, (h + 1) * DH)
            y = norm(hsum[:, sl]) * nw_ref[:, off + h * DH:off + (h + 1) * DH] * gate[:, sl]
            parts.append(y.astype(BF16))
    mix = _dot(jnp.concatenate(parts, axis=-1), wout_ref[...])
    o_ref[...] = _layer_norm(ALPHA * x_ref[...] + mix, lg_ref[...], lb_ref[...])


def _mix_out(norms, h1f, h1b, g1, h2f, h2b, g2, x2d, nw, w_out, lg, lb):
    t = x2d.shape[0]
    tm = min(PROJ_TM, t)

    def tile(w):
        return pl.BlockSpec((tm, w), lambda i: (i, 0))

    flat = [a.reshape(t, MIX_W) for a in (h1f, h1b, g1, h2f, h2b, g2)]
    return pl.pallas_call(
        functools.partial(_mix_out_kernel, norms),
        grid=(t // tm,),
        in_specs=[tile(MIX_W)] * 6 + [tile(D_MODEL), _full_spec(nw.shape), _full_spec(w_out.shape),
                                      _full_spec(lg.shape), _full_spec(lb.shape)],
        out_specs=tile(D_MODEL),
        out_shape=jax.ShapeDtypeStruct((t, D_MODEL), F32),
        compiler_params=_params(1),
        name="mix_out",
    )(*flat, x2d, nw, w_out, lg, lb)


def _pad_cols(w, width):
    return jnp.pad(w, ((0, 0), (0, width - w.shape[1])))


def _even_weights(w_in, gate_b, conv_w, a2_w, a2_b):
    sizes = (A_W, A_W, A_W, A_W, 4 * HEADS, B_KW, B_KW, MIX_W, MIX_W, 2 * B_RANK)
    offs = np.concatenate([[0], np.cumsum(sizes)])
    aq, ak, av, ao, ag, bq, bk, bv, bg, ba = [w_in[:, offs[i]:offs[i + 1]] for i in range(len(sizes))]
    wqk = jnp.concatenate([aq, ak], axis=1).astype(BF16)
    wrest = jnp.concatenate([av, ao, bq, bk, bv, bg, _pad_cols(ag, LANES), _pad_cols(ba, LANES)], axis=1).astype(BF16)
    convw = jnp.pad(conv_w, ((0, SUBLANES - CONV_W), (0, 0)))
    gateb = _pad_cols(gate_b[None, :], LANES)
    a2w = jnp.zeros((LANES, 2 * B_KW), F32)
    a2w = a2w.at[0:B_RANK, 0:B_KW].set(a2_w[0]).at[B_RANK:2 * B_RANK, B_KW:].set(a2_w[1]).astype(BF16)
    a2b = jnp.concatenate([a2_b[0], a2_b[1]])[None, :]
    return wqk, wrest, convw, gateb, a2w, a2b


def _rotary_tables(n):
    inv = 1.0 / (ROPE_BASE ** (jnp.arange(0, DH, 2, dtype=F32) / DH))
    ang = jnp.arange(n, dtype=F32)[:, None] * inv[None, :]
    cos, sin = jnp.cos(ang), jnp.sin(ang)
    return jnp.concatenate([cos, cos], -1), jnp.concatenate([-sin, sin], -1)


def _trunk(x, p):
    bsz, n, _ = x.shape
    t = bsz * n
    x2 = x.reshape(t, D_MODEL)
    cos2, sin2 = _rotary_tables(n)
    for l in range(DEPTH):
        j = l // 2
        x2 = _ffn_ln(x2, p["ffn1_in"][l], p["ffn1_out"][l], p["ln_g"][l, 0][None], p["ln_b"][l, 0][None])
        x3 = x2.reshape(bsz, n, D_MODEL)
        if l % 2 == 0:
            qa, ka, va, oga, gates, qb, kb, vb, gb, loga = _even_in(x3, *p["even"][j])
            h1f, h1b = _mlstm(qa, ka, va, gates)
            h2f, h2b = _vec_scan(qb, kb, kb, vb, loga[..., :B_KW], loga[..., B_KW:], B_DK, "gla_scan")
            norms, g1, g2 = (_head_ln, _head_rms), oga, gb
            nw, w_out = p["ev_nw"][j], p["ev_out"][j]
        else:
            lb = _hgrn_lower_bound(p["od_lb_logits"], j)
            qc, lff, lfb, kcf, kcb, vc, gc, qr, kr, vr, gr = _odd_in(x3, p["od_in"][j], lb, cos2, sin2)
            h1f, h1b = _vec_scan(qc, kcf, kcb, vc, lff, lfb, DH, "hgrn_scan")
            h2f, h2b = _retention(qr, kr, vr)
            norms, g1, g2 = (_head_rms, _head_ln), gc, gr
            nw, w_out = p["od_nw"][j], p["od_out"][j]
        x2 = _mix_out(norms, h1f, h1b, g1, h2f, h2b, g2, x2, nw, w_out,
                      p["ln_g"][l, 1][None], p["ln_b"][l, 1][None])
        x2 = _ffn_ln(x2, p["ffn2_in"][l], p["ffn2_out"][l], p["ln_g"][l, 2][None], p["ln_b"][l, 2][None])
    return x2.reshape(bsz, n, D_MODEL)


def kernel(x_prompt, x_sample, ffn1_w_in, ffn1_w_out, ffn2_w_in, ffn2_w_out, ln_g, ln_b, ev_w_in, ev_gate_b, ev_conv_w, ev_gla_a2_w, ev_gla_a2_b, ev_norm_w, ev_w_out, od_w_in, od_lb_logits, od_norm_w, od_w_out):
    n_even, n_odd = ev_w_in.shape[0], od_w_in.shape[0]
    p = {
        "ffn1_in": ffn1_w_in.astype(BF16), "ffn1_out": ffn1_w_out.astype(BF16),
        "ffn2_in": ffn2_w_in.astype(BF16), "ffn2_out": ffn2_w_out.astype(BF16),
        "ln_g": ln_g, "ln_b": ln_b,
        "even": [_even_weights(ev_w_in[j], ev_gate_b[j], ev_conv_w[j], ev_gla_a2_w[j], ev_gla_a2_b[j])
                 for j in range(n_even)],
        "ev_nw": [ev_norm_w[j][None, :] for j in range(n_even)],
        "ev_out": ev_w_out.astype(BF16),
        "od_in": od_w_in.astype(BF16),
        "od_lb_logits": od_lb_logits,
        "od_nw": [jnp.concatenate([od_norm_w[j], jnp.ones((MIX_W,), F32)])[None, :] for j in range(n_odd)],
        "od_out": od_w_out.astype(BF16),
    }
    return (_trunk(x_prompt, p), _trunk(x_sample, p))
```

```python
import functools
import math

import numpy as np
import jax
import jax.numpy as jnp
from jax import lax
from jax.experimental import pallas as pl
from jax.experimental.pallas import tpu as pltpu

F32 = jnp.float32
BF16 = jnp.bfloat16

D_MODEL = 1024
DEPTH = 4
D_FF = 2816
LN_EPS = 1e-5
ALPHA = (2 * DEPTH) ** 0.25
NEG_BIG = -1e30
TINY = 1e-30
LOG2E = math.log2(math.e)

HEADS = 4
DH = 128
MIX_W = HEADS * DH
A_W = MIX_W
CONV_W = 5
B_DK = 64
B_KW = HEADS * B_DK
B_RANK = 16
GLA_GATE_NORMALIZER = 16.0
ROPE_BASE = 10000.0

LANES = 128
SUBLANES = 8
VMEM_LIMIT = 56 * 1024 * 1024

FFN_TM = 512
FFN_FC = 256
PROJ_TM = 512
SCALAR_CHUNK = 256
VEC_CHUNK = 128
VEC_SUB = SUBLANES


def _dot(a, b):
    return jnp.dot(a, b, preferred_element_type=F32)


def _dot_nt(a, b):
    return lax.dot_general(a, b, (((1,), (1,)), ((), ())), preferred_element_type=F32)


def _dot_tn(a, b):
    return lax.dot_general(a, b, (((0,), (0,)), ((), ())), preferred_element_type=F32)


def _split3(x):
    hi = x.astype(BF16)
    r1 = x - hi.astype(F32)
    mid = r1.astype(BF16)
    lo = (r1 - mid.astype(F32)).astype(BF16)
    return hi, mid, lo


def _dot_exact(w_bf16, x, terms=3):
    hi = x.astype(BF16)
    r1 = x - hi.astype(F32)
    mid = r1.astype(BF16)
    out = _dot(w_bf16, hi) + _dot(w_bf16, mid)
    if terms == 3:
        out = out + _dot(w_bf16, (r1 - mid.astype(F32)).astype(BF16))
    return out


def _sigmoid(x):
    return 1.0 / (1.0 + jnp.exp(-x))


def _silu(x):
    return x * _sigmoid(x)


def _log_sigmoid(x):
    return jnp.minimum(x, 0.0) - jnp.log1p(jnp.exp(-jnp.abs(x)))


def _layer_norm(y, g, b):
    mu = jnp.mean(y, -1, keepdims=True)
    d = y - mu
    var = jnp.mean(d * d, -1, keepdims=True)
    return d * lax.rsqrt(var + LN_EPS) * g + b


def _head_ln(x):
    mu = jnp.mean(x, -1, keepdims=True)
    d = x - mu
    var = jnp.mean(d * d, -1, keepdims=True)
    return d * lax.rsqrt(var + LN_EPS)


def _head_rms(x):
    return x * lax.rsqrt(jnp.mean(x * x, -1, keepdims=True) + LN_EPS)


def _params(n_grid):
    return pltpu.CompilerParams(dimension_semantics=("arbitrary",) * n_grid,
                                vmem_limit_bytes=VMEM_LIMIT)


def _full_spec(shape):
    nd = len(shape)
    return pl.BlockSpec(shape, lambda *_: (0,) * nd)


def _lane_tile(x, width):
    return jnp.concatenate([x] * (width // x.shape[-1]), axis=-1)


def _ffn_kernel(x_ref, win_ref, wout_ref, g_ref, b_ref, o_ref):
    x = x_ref[...]
    xb = x.astype(BF16)
    acc = jnp.zeros(x.shape, F32)
    for j in range(D_FF // FFN_FC):
        lo = j * FFN_FC
        g = _dot(xb, win_ref[:, lo:lo + FFN_FC])
        u = _dot(xb, win_ref[:, D_FF + lo:D_FF + lo + FFN_FC])
        h = (_silu(g) * u).astype(BF16)
        acc = acc + _dot(h, wout_ref[lo:lo + FFN_FC, :])
    o_ref[...] = _layer_norm(ALPHA * x + 0.5 * acc, g_ref[...], b_ref[...])


def _ffn_ln(x2d, w_in, w_out, g, b):
    t = x2d.shape[0]
    tm = min(FFN_TM, t)
    return pl.pallas_call(
        _ffn_kernel,
        grid=(t // tm,),
        in_specs=[pl.BlockSpec((tm, D_MODEL), lambda i: (i, 0)),
                  _full_spec(w_in.shape), _full_spec(w_out.shape),
                  _full_spec(g.shape), _full_spec(b.shape)],
        out_specs=pl.BlockSpec((tm, D_MODEL), lambda i: (i, 0)),
        out_shape=jax.ShapeDtypeStruct((t, D_MODEL), F32),
        compiler_params=_params(1),
        name="ffn_ln",
    )(x2d, w_in, w_out, g, b)


_EV_OFF = {"av": 0, "ao": 512, "bq": 1024, "bk": 1280, "bv": 1536, "bg": 2048, "ag": 2560, "ba": 2688}
HALO = SUBLANES


def _even_in_kernel(xp_ref, xc_ref, xn_ref, wqk_ref, wrest_ref, convw_ref, gateb_ref, a2w_ref, a2b_ref,
                    qa_ref, ka_ref, va_ref, oga_ref, gates_ref, qb_ref, kb_ref, vb_ref, gb_ref,
                    logaf_ref, logab_ref, ybuf):
    i = pl.program_id(1)
    last = pl.num_programs(1) - 1
    tm = xc_ref.shape[1]
    xc = xc_ref[0].astype(BF16)
    xp = jnp.where(i > 0, xp_ref[0], 0.0).astype(BF16)
    xn = jnp.where(i < last, xn_ref[0], 0.0).astype(BF16)
    wqk = wqk_ref[...]
    ybuf[0:HALO, :] = _dot(xp, wqk)
    ybuf[HALO:HALO + tm, :] = _dot(xc, wqk)
    ybuf[HALO + tm:2 * HALO + tm, :] = _dot(xn, wqk)
    base = HALO - CONV_W // 2
    acc = ybuf[pl.ds(base, tm), :] * convw_ref[0:1, :]
    for j in range(1, CONV_W):
        acc = acc + ybuf[pl.ds(base + j, tm), :] * convw_ref[j:j + 1, :]
    qk = _silu(acc)
    qa_ref[0] = qk[:, :A_W].astype(BF16)
    ka_ref[0] = (qk[:, A_W:] * DH ** -0.5).astype(BF16)

    def sect(name, width):
        lo = _EV_OFF[name]
        return _dot(xc, wrest_ref[:, lo:lo + width])

    va_ref[0] = sect("av", MIX_W).astype(BF16)
    oga_ref[0] = _sigmoid(sect("ao", MIX_W)).astype(BF16)
    qb_ref[0] = (sect("bq", B_KW) * B_DK ** -0.5).astype(BF16)
    kb_ref[0] = sect("bk", B_KW).astype(BF16)
    vb_ref[0] = sect("bv", MIX_W).astype(BF16)
    gb_ref[0] = _silu(sect("bg", MIX_W)).astype(BF16)
    gates = sect("ag", LANES) + gateb_ref[...]
    col = lax.broadcasted_iota(jnp.int32, gates.shape, 1)
    is_f = (col % 8) >= 4
    gates_ref[0] = jnp.where(is_f, _log_sigmoid(gates), gates)
    ba = sect("ba", LANES).astype(BF16)
    a_pre = _dot(ba, a2w_ref[...]) + a2b_ref[...]
    log_a = _log_sigmoid(a_pre) * (1.0 / GLA_GATE_NORMALIZER)
    logaf_ref[0] = log_a[:, :B_KW]
    logab_ref[0] = log_a[:, B_KW:]


def _even_in(x, wqk, wrest, convw, gateb, a2w, a2b):
    bsz, n, _ = x.shape
    tm = min(PROJ_TM, n)
    nt = n // tm
    r = tm // HALO
    nblk8 = n // HALO

    def tile(w, dt):
        return (pl.BlockSpec((1, tm, w), lambda b, i: (b, i, 0)), jax.ShapeDtypeStruct((bsz, n, w), dt))

    outs = [tile(A_W, BF16), tile(A_W, BF16), tile(MIX_W, BF16), tile(MIX_W, BF16), tile(LANES, F32),
            tile(B_KW, BF16), tile(B_KW, BF16), tile(MIX_W, BF16), tile(MIX_W, BF16),
            tile(B_KW, F32), tile(B_KW, F32)]
    return pl.pallas_call(
        _even_in_kernel,
        grid=(bsz, nt),
        in_specs=[pl.BlockSpec((1, HALO, D_MODEL), lambda b, i: (b, jnp.maximum(i * r - 1, 0), 0)),
                  pl.BlockSpec((1, tm, D_MODEL), lambda b, i: (b, i, 0)),
                  pl.BlockSpec((1, HALO, D_MODEL), lambda b, i: (b, jnp.minimum((i + 1) * r, nblk8 - 1), 0)),
                  _full_spec(wqk.shape), _full_spec(wrest.shape), _full_spec(convw.shape),
                  _full_spec(gateb.shape), _full_spec(a2w.shape), _full_spec(a2b.shape)],
        out_specs=[o[0] for o in outs],
        out_shape=[o[1] for o in outs],
        scratch_shapes=[pltpu.VMEM((tm + 2 * HALO, 2 * A_W), F32)],
        compiler_params=_params(2),
        name="even_in",
    )(x, x, x, wqk, wrest, convw, gateb, a2w, a2b)


def _odd_in_kernel(x_ref, w_ref, lb_ref, cos_ref, sin_ref,
                   qc_ref, lff_ref, lfb_ref, kcf_ref, kcb_ref, vc_ref, gc_ref,
                   qr_ref, kr_ref, vr_ref, gr_ref):
    xb = x_ref[0].astype(BF16)

    def sect(idx):
        return _dot(xb, w_ref[:, idx * MIX_W:(idx + 1) * MIX_W])

    qc_ref[0] = (_silu(sect(0)) * DH ** -0.5).astype(BF16)
    for d, (lf_ref, k_ref) in enumerate(((lff_ref, kcf_ref), (lfb_ref, kcb_ref))):
        z = sect(1 + d)
        lb = lb_ref[d:d + 1, :]
        f = lb + (1.0 - lb) * _sigmoid(z)
        lf_ref[0] = jnp.log(jnp.maximum(f, TINY))
        k_ref[0] = ((1.0 - lb) * _sigmoid(-z)).astype(BF16)
    vc_ref[0] = sect(3).astype(BF16)
    gc_ref[0] = _silu(sect(4)).astype(BF16)
    cos2 = cos_ref[...]
    sin2 = sin_ref[...]

    def rotary(y, scale):
        parts = []
        for h in range(HEADS):
            yh = y[:, h * DH:(h + 1) * DH]
            parts.append((yh * cos2 + pltpu.roll(yh, DH // 2, axis=1) * sin2) * scale)
        return jnp.concatenate(parts, axis=-1).astype(BF16)

    qr_ref[0] = rotary(sect(5), 1.0)
    kr_ref[0] = rotary(sect(6), DH ** -0.5)
    vr_ref[0] = sect(7).astype(BF16)
    gr_ref[0] = _silu(sect(8)).astype(BF16)


def _odd_in(x, w, lb, cos2, sin2):
    bsz, n, _ = x.shape
    tm = min(PROJ_TM, n)

    def tile(dt):
        return (pl.BlockSpec((1, tm, MIX_W), lambda b, i: (b, i, 0)), jax.ShapeDtypeStruct((bsz, n, MIX_W), dt))

    dts = [BF16, F32, F32, BF16, BF16, BF16, BF16, BF16, BF16, BF16, BF16]
    outs = [tile(dt) for dt in dts]
    return pl.pallas_call(
        _odd_in_kernel,
        grid=(bsz, n // tm),
        in_specs=[pl.BlockSpec((1, tm, D_MODEL), lambda b, i: (b, i, 0)),
                  _full_spec(w.shape), _full_spec(lb.shape),
                  pl.BlockSpec((tm, DH), lambda b, i: (i, 0)),
                  pl.BlockSpec((tm, DH), lambda b, i: (i, 0))],
        out_specs=[o[0] for o in outs],
        out_shape=[o[1] for o in outs],
        compiler_params=_params(2),
        name="odd_in",
    )(x, w, lb, cos2, sin2)


def _hgrn_lower_bound(lb_logits, layer_idx):
    n_odd = lb_logits.shape[0]

    def kern(l_ref, o_ref):
        rows = [l_ref[i] for i in range(n_odd)]
        mx = rows[0]
        for r in rows[1:]:
            mx = jnp.maximum(mx, r)
        es = [jnp.exp(r - mx) for r in rows]
        tot = es[0]
        for e in es[1:]:
            tot = tot + e
        acc = jnp.zeros_like(tot)
        for i in range(1, layer_idx + 1):
            acc = acc + es[i] / tot
        o_ref[...] = acc

    return pl.pallas_call(
        kern,
        out_shape=jax.ShapeDtypeStruct(lb_logits.shape[1:], F32),
        name="hgrn_lb",
    )(lb_logits)


def _mlstm_kernel(cum_ref, sel_ref, qf_ref, kf_ref, vf_ref, gf_ref, qb_ref, kb_ref, vb_ref, gb_ref,
                  of_ref, ob_ref, c_ref, m_ref):
    c = qf_ref.shape[1]

    @pl.when(pl.program_id(1) == 0)
    def _():
        c_ref[...] = jnp.zeros_like(c_ref)
        m_ref[...] = jnp.zeros_like(m_ref)

    row = lax.broadcasted_iota(jnp.int32, (c, c), 0)
    colm = lax.broadcasted_iota(jnp.int32, (c, c), 1)
    ones_blk = jnp.ones((c, DH), BF16)
    dirs = ((qf_ref, kf_ref, vf_ref, gf_ref, of_ref, colm <= row, c - 1),
            (qb_ref, kb_ref, vb_ref, gb_ref, ob_ref, colm >= row, 0))
    for d, (q_ref, k_ref, v_ref, g_ref, o_ref, mask, end) in enumerate(dirs):
        gates = g_ref[0] * LOG2E
        cum = _dot_exact(cum_ref[d], gates)
        xs = jnp.concatenate(_split3(jnp.concatenate([gates, cum], axis=-1)), axis=-1)
        gates_t = gates.T
        cum_t = cum.T
        for h in range(HEADS):
            ci, cf = 8 * d + h, 8 * d + 4 + h
            sl = slice(h * DH, (h + 1) * DH)
            s = d * HEADS + h
            q = q_ref[0, :, sl]
            k = k_ref[0, :, sl]
            v_ext = jnp.concatenate([v_ref[0, :, sl], ones_blk], axis=-1)
            rep = _dot(xs, sel_ref[s])
            i_rep, b_rep = rep[:, :DH], rep[:, DH:]
            r_row = gates_t[ci:ci + 1, :] - cum_t[cf:cf + 1, :]
            tot = b_rep[end:end + 1, :]
            m_old = m_ref[s:s + 1, :]
            c_old = c_ref[s]
            dmat = jnp.where(mask, _lane_tile(b_rep, c) + r_row, NEG_BIG)
            m_inter = b_rep + m_old
            m_t = jnp.maximum(m_inter, jnp.max(dmat, axis=-1, keepdims=True))
            att = (_dot_nt(q, k) * jnp.exp2(dmat - _lane_tile(m_t, c))).astype(BF16)
            sc = jnp.exp2(m_inter - m_t)
            ext = _lane_tile(sc, 2 * DH) * _dot(q, c_old.astype(BF16)) + _dot(att, v_ext)
            num, den = ext[:, :DH], ext[:, DH:]
            o_ref[0, :, sl] = num / jnp.maximum(jnp.abs(den), jnp.exp2(-m_t))
            g_end = tot + i_rep - b_rep
            m_new = jnp.maximum(tot + m_old, jnp.max(g_end, axis=0, keepdims=True))
            decay = jnp.exp2(tot + m_old - m_new)
            wk = (k.astype(F32) * jnp.exp2(g_end - m_new)).astype(BF16)
            c_ref[s] = _lane_tile(decay, 2 * DH) * c_old + _dot_tn(wk, v_ext)
            m_ref[s:s + 1, :] = m_new


def _mlstm_consts(c):
    low = np.tril(np.ones((c, c), np.float32))
    cum = np.stack([low, low.T])
    sel = np.zeros((2 * HEADS, 3 * 2 * LANES, 2 * DH), np.float32)
    for d in range(2):
        for h in range(HEADS):
            for part in range(3):
                sel[d * HEADS + h, part * 2 * LANES + 8 * d + h, :DH] = 1.0
                sel[d * HEADS + h, part * 2 * LANES + LANES + 8 * d + 4 + h, DH:] = 1.0
    return jnp.asarray(cum, BF16), jnp.asarray(sel, BF16)


def _mlstm(q, k, v, gates):
    bsz, n, _ = q.shape
    c = min(SCALAR_CHUNK, n)
    nc = n // c

    def fwd(w):
        return pl.BlockSpec((1, c, w), lambda b, i: (b, i, 0))

    def bwd(w):
        return pl.BlockSpec((1, c, w), lambda b, i: (b, nc - 1 - i, 0))

    cum, sel = _mlstm_consts(c)
    out = jax.ShapeDtypeStruct((bsz, n, MIX_W), F32)
    return pl.pallas_call(
        _mlstm_kernel,
        grid=(bsz, nc),
        in_specs=[_full_spec(cum.shape), _full_spec(sel.shape),
                  fwd(MIX_W), fwd(MIX_W), fwd(MIX_W), fwd(LANES),
                  bwd(MIX_W), bwd(MIX_W), bwd(MIX_W), bwd(LANES)],
        out_specs=[fwd(MIX_W), bwd(MIX_W)],
        out_shape=[out, out],
        scratch_shapes=[pltpu.VMEM((2 * HEADS, DH, 2 * DH), F32),
                        pltpu.VMEM((2 * HEADS, LANES), F32)],
        compiler_params=_params(2),
        name="mlstm_scan",
    )(cum, sel, q, k, v, gates, q, k, v, gates)


def _vec_levels(c):
    lv, h = [], VEC_SUB
    while h < c:
        lv.append(h)
        h *= 2
    return lv


def _vec_consts(c, hpg):
    levels = _vec_levels(c)
    t = np.arange(c)
    low = np.tril(np.ones((c, c), np.float32))
    lev = np.full((c, c), -2, np.int32)
    same_sub = (t[:, None] // VEC_SUB) == (t[None, :] // VEC_SUB)
    lev[same_sub & (t[None, :] <= t[:, None])] = -1
    for li, h in enumerate(levels):
        blk = t // (2 * h)
        second = (t // h) % 2 == 1
        pair = (blk[:, None] == blk[None, :]) & second[:, None] & (~second)[None, :]
        lev[pair] = li
    wall = jnp.asarray(np.stack([low, low.T]), BF16)
    levid = np.stack([np.tile(lev, (1, hpg)), np.tile(lev[::-1, ::-1], (1, hpg))])
    kmask = np.zeros((hpg * c, VEC_SUB * LANES), np.float32)
    lane_head = np.arange(LANES) // (LANES // hpg)
    for hh in range(hpg):
        for j in range(VEC_SUB):
            rows = hh * c + t[t % VEC_SUB == j]
            kmask[np.ix_(rows, j * LANES + np.nonzero(lane_head == hh)[0])] = 1.0
    return wall, jnp.asarray(levid, jnp.int32), jnp.asarray(kmask, BF16), tuple(levels)


def _level_exponent(b, h, reverse):
    pieces = []
    for r0 in range(0, b.shape[0], 2 * h):
        ref = r0 + h if reverse else r0 + h - 1
        bm = b[ref:ref + 1, :]
        first, second = b[r0:r0 + h], b[r0 + h:r0 + 2 * h]
        pieces += [first - bm, bm - second] if reverse else [bm - first, second - bm]
    return jnp.concatenate(pieces, axis=0)


def _vec_kernel(hpg, levels, wall_ref, lev_ref, kmask_ref, q_f, k_f, v_f, g_f, q_b, k_b, v_b, g_b,
                of_ref, ob_ref, st_ref):
    c = q_f.shape[1]
    ngrp = q_f.shape[2] // LANES
    sub = VEC_SUB

    @pl.when(pl.program_id(1) == 0)
    def _():
        st_ref[...] = jnp.zeros_like(st_ref)

    lane = lax.broadcasted_iota(jnp.int32, (1, LANES), 1)
    head_of_lane = lane // (LANES // hpg)

    def hsel(x, hh):
        return x if hpg == 1 else jnp.where(head_of_lane == hh, x, 0.0)

    kmask = kmask_ref[...]
    dirs = ((q_f, k_f, v_f, g_f, of_ref, c - 1), (q_b, k_b, v_b, g_b, ob_ref, 0))
    for d, (q_ref, k_ref, v_ref, g_ref, o_ref, end) in enumerate(dirs):
        b_all = _dot_exact(wall_ref[d], g_ref[0] * LOG2E, terms=2)
        lev = lev_ref[d]
        for grp in range(ngrp):
            ls = slice(grp * LANES, (grp + 1) * LANES)
            s = d * ngrp + grp
            q = q_ref[0, :, ls].astype(F32)
            kb = k_ref[0, :, ls]
            k = kb.astype(F32)
            b = b_all[:, ls]
            st = st_ref[s]
            b3 = b.reshape(c // sub, sub, LANES)
            parts = []
            for j in range(sub):
                bj = jnp.broadcast_to(b3[:, j:j + 1, :], b3.shape).reshape(c, LANES)
                parts.append((q * jnp.exp2(jnp.minimum(b - bj, 0.0))).astype(BF16))
            kspread = jnp.tile(kb, (hpg, sub)) * kmask
            a = jnp.where(lev == -1, _dot_nt(jnp.concatenate(parts, axis=-1), kspread), 0.0)
            for li, h in enumerate(levels):
                e = jnp.exp2(_level_exponent(b, h, d == 1))
                ke = k * e
                ke = jnp.concatenate([hsel(ke, hh) for hh in range(hpg)], axis=0).astype(BF16)
                a = jnp.where(lev == li, _dot_nt((q * e).astype(BF16), ke), a)
            a = a.astype(BF16)
            qe0 = q * jnp.exp2(b)
            b_end = b[end:end + 1, :]
            kw = k * jnp.exp2(b_end - b)
            stb = st.astype(BF16)
            upd = None
            for hh in range(hpg):
                head = grp * hpg + hh
                vs = slice(head * DH, (head + 1) * DH)
                v = v_ref[0, :, vs]
                o_ref[0, :, vs] = (_dot_nt(hsel(qe0, hh).astype(BF16), stb)
                                   + _dot(a[:, hh * c:(hh + 1) * c], v))
                u = _dot_tn(v, hsel(kw, hh).astype(BF16))
                upd = u if upd is None else upd + u
            st_ref[s] = st * jnp.exp2(b_end) + upd


def _vec_scan(q, k_f, k_b, v, g_f, g_b, hpg, name):
    bsz, n, kw = q.shape
    c = min(VEC_CHUNK, n)
    nc = n // c
    wall, levid, kmask, levels = _vec_consts(c, hpg)

    def fwd(w):
        return pl.BlockSpec((1, c, w), lambda b, i: (b, i, 0))

    def bwd(w):
        return pl.BlockSpec((1, c, w), lambda b, i: (b, nc - 1 - i, 0))

    out = jax.ShapeDtypeStruct((bsz, n, MIX_W), F32)
    return pl.pallas_call(
        functools.partial(_vec_kernel, hpg, levels),
        grid=(bsz, nc),
        in_specs=[_full_spec(wall.shape), _full_spec(levid.shape), _full_spec(kmask.shape),
                  fwd(kw), fwd(kw), fwd(MIX_W), fwd(kw),
                  bwd(kw), bwd(kw), bwd(MIX_W), bwd(kw)],
        out_specs=[fwd(MIX_W), bwd(MIX_W)],
        out_shape=[out, out],
        scratch_shapes=[pltpu.VMEM((2 * kw // LANES, DH, LANES), F32)],
        compiler_params=_params(2),
        name=name,
    )(wall, levid, kmask, q, k_f, v, g_f, q, k_b, v, g_b)


def _ret_consts(c):
    log_gamma = np.log1p(-(2.0 ** (-5.0 - np.arange(HEADS, dtype=np.float64))))
    t = np.arange(c, dtype=np.float64)
    diff = t[:, None] - t[None, :]
    dmat = np.zeros((2, HEADS, c, c))
    qdec = np.zeros((2, c, MIX_W))
    kdec = np.zeros((2, c, MIX_W))
    for h in range(HEADS):
        lg = log_gamma[h]
        dmat[0, h] = np.where(diff >= 0, np.exp(lg * np.maximum(diff, 0)), 0.0)
        dmat[1, h] = dmat[0, h].T
        sl = slice(h * DH, (h + 1) * DH)
        qdec[0, :, sl] = np.exp(lg * (t + 1))[:, None]
        qdec[1, :, sl] = np.exp(lg * (c - t))[:, None]
        kdec[0, :, sl] = np.exp(lg * (c - 1 - t))[:, None]
        kdec[1, :, sl] = np.exp(lg * t)[:, None]
    sdec = [float(np.exp(lg * c)) for lg in log_gamma]
    return (jnp.asarray(dmat, F32), jnp.asarray(qdec, F32), jnp.asarray(kdec, F32)), sdec


def _ret_kernel(sdec, dmat_ref, qdec_ref, kdec_ref, q_f, k_f, v_f, q_b, k_b, v_b, of_ref, ob_ref, st_ref):
    @pl.when(pl.program_id(1) == 0)
    def _():
        st_ref[...] = jnp.zeros_like(st_ref)

    dirs = ((q_f, k_f, v_f, of_ref), (q_b, k_b, v_b, ob_ref))
    for d, (q_ref, k_ref, v_ref, o_ref) in enumerate(dirs):
        for h in range(HEADS):
            sl = slice(h * DH, (h + 1) * DH)
            s = d * HEADS + h
            q = q_ref[0, :, sl]
            k = k_ref[0, :, sl]
            v = v_ref[0, :, sl]
            st = st_ref[s]
            att = _dot_nt(q, k) * dmat_ref[d, h]
            qd = (q.astype(F32) * qdec_ref[d, :, sl]).astype(BF16)
            o_ref[0, :, sl] = _dot(qd, st.astype(BF16)) + _dot(att.astype(BF16), v)
            kd = (k.astype(F32) * kdec_ref[d, :, sl]).astype(BF16)
            st_ref[s] = sdec[h] * st + _dot_tn(kd, v)


def _retention(q, k, v):
    bsz, n, _ = q.shape
    c = min(SCALAR_CHUNK, n)
    nc = n // c
    consts, sdec = _ret_consts(c)

    def fwd(w):
        return pl.BlockSpec((1, c, w), lambda b, i: (b, i, 0))

    def bwd(w):
        return pl.BlockSpec((1, c, w), lambda b, i: (b, nc - 1 - i, 0))

    out = jax.ShapeDtypeStruct((bsz, n, MIX_W), F32)
    return pl.pallas_call(
        functools.partial(_ret_kernel, sdec),
        grid=(bsz, nc),
        in_specs=[_full_spec(a.shape) for a in consts] + [fwd(MIX_W)] * 3 + [bwd(MIX_W)] * 3,
        out_specs=[fwd(MIX_W), bwd(MIX_W)],
        out_shape=[out, out],
        scratch_shapes=[pltpu.VMEM((2 * HEADS, DH, DH), F32)],
        compiler_params=_params(2),
        name="retention_scan",
    )(*consts, q, k, v, q, k, v)


def _mix_out_kernel(norms, h1f_ref, h1b_ref, g1_ref, h2f_ref, h2b_ref, g2_ref, x_ref,
                    nw_ref, wout_ref, lg_ref, lb_ref, o_ref):
    parts = []
    groups = ((h1f_ref, h1b_ref, g1_ref, norms[0], 0), (h2f_ref, h2b_ref, g2_ref, norms[1], MIX_W))
    for hf_ref, hb_ref, gate_ref, norm, off in groups:
        hsum = hf_ref[...] + hb_ref[...]
        gate = gate_ref[...].astype(F32)
        for h in range(HEADS):
            sl = slice(h * DH, (h + 1) * DH)
            y = norm(hsum[:, sl]) * nw_ref[:, off + h * DH:off + (h + 1) * DH] * gate[:, sl]
            parts.append(y.astype(BF16))
    mix = _dot(jnp.concatenate(parts, axis=-1), wout_ref[...])
    o_ref[...] = _layer_norm(ALPHA * x_ref[...] + mix, lg_ref[...], lb_ref[...])


def _mix_out(norms, h1f, h1b, g1, h2f, h2b, g2, x2d, nw, w_out, lg, lb):
    t = x2d.shape[0]
    tm = min(PROJ_TM, t)

    def tile(w):
        return pl.BlockSpec((tm, w), lambda i: (i, 0))

    flat = [a.reshape(t, MIX_W) for a in (h1f, h1b, g1, h2f, h2b, g2)]
    return pl.pallas_call(
        functools.partial(_mix_out_kernel, norms),
        grid=(t // tm,),
        in_specs=[tile(MIX_W)] * 6 + [tile(D_MODEL), _full_spec(nw.shape), _full_spec(w_out.shape),
                                      _full_spec(lg.shape), _full_spec(lb.shape)],
        out_specs=tile(D_MODEL),
        out_shape=jax.ShapeDtypeStruct((t, D_MODEL), F32),
        compiler_params=_params(1),
        name="mix_out",
    )(*flat, x2d, nw, w_out, lg, lb)


def _pad_cols(w, width):
    return jnp.pad(w, ((0, 0), (0, width - w.shape[1])))


def _even_weights(w_in, gate_b, conv_w, a2_w, a2_b):
    sizes = (A_W, A_W, A_W, A_W, 4 * HEADS, B_KW, B_KW, MIX_W, MIX_W, 2 * B_RANK)
    offs = np.concatenate([[0], np.cumsum(sizes)])
    aq, ak, av, ao, ag, bq, bk, bv, bg, ba = [w_in[:, offs[i]:offs[i + 1]] for i in range(len(sizes))]
    wqk = jnp.concatenate([aq, ak], axis=1).astype(BF16)
    wrest = jnp.concatenate([av, ao, bq, bk, bv, bg, _pad_cols(ag, LANES), _pad_cols(ba, LANES)], axis=1).astype(BF16)
    convw = jnp.pad(conv_w, ((0, SUBLANES - CONV_W), (0, 0)))
    gateb = _pad_cols(gate_b[None, :], LANES)
    a2w = jnp.zeros((LANES, 2 * B_KW), F32)
    a2w = a2w.at[0:B_RANK, 0:B_KW].set(a2_w[0]).at[B_RANK:2 * B_RANK, B_KW:].set(a2_w[1]).astype(BF16)
    a2b = jnp.concatenate([a2_b[0], a2_b[1]])[None, :]
    return wqk, wrest, convw, gateb, a2w, a2b


def _rotary_tables(n):
    inv = 1.0 / (ROPE_BASE ** (jnp.arange(0, DH, 2, dtype=F32) / DH))
    ang = jnp.arange(n, dtype=F32)[:, None] * inv[None, :]
    cos, sin = jnp.cos(ang), jnp.sin(ang)
    return jnp.concatenate([cos, cos], -1), jnp.concatenate([-sin, sin], -1)


def _trunk(x, p):
    bsz, n, _ = x.shape
    t = bsz * n
    x2 = x.reshape(t, D_MODEL)
    cos2, sin2 = _rotary_tables(n)
    for l in range(DEPTH):
        j = l // 2
        x2 = _ffn_ln(x2, p["ffn1_in"][l], p["ffn1_out"][l], p["ln_g"][l, 0][None], p["ln_b"][l, 0][None])
        x3 = x2.reshape(bsz, n, D_MODEL)
        if l % 2 == 0:
            qa, ka, va, oga, gates, qb, kb, vb, gb, logaf, logab = _even_in(x3, *p["even"][j])
            h1f, h1b = _mlstm(qa, ka, va, gates)
            h2f, h2b = _vec_scan(qb, kb, kb, vb, logaf, logab, DH // B_DK, "gla_scan")
            norms, g1, g2 = (_head_ln, _head_rms), oga, gb
            nw, w_out = p["ev_nw"][j], p["ev_out"][j]
        else:
            lb = _hgrn_lower_bound(p["od_lb_logits"], j)
            qc, lff, lfb, kcf, kcb, vc, gc, qr, kr, vr, gr = _odd_in(x3, p["od_in"][j], lb, cos2, sin2)
            h1f, h1b = _vec_scan(qc, kcf, kcb, vc, lff, lfb, 1, "hgrn_scan")
            h2f, h2b = _retention(qr, kr, vr)
            norms, g1, g2 = (_head_rms, _head_ln), gc, gr
            nw, w_out = p["od_nw"][j], p["od_out"][j]
        x2 = _mix_out(norms, h1f, h1b, g1, h2f, h2b, g2, x2, nw, w_out,
                      p["ln_g"][l, 1][None], p["ln_b"][l, 1][None])
        x2 = _ffn_ln(x2, p["ffn2_in"][l], p["ffn2_out"][l], p["ln_g"][l, 2][None], p["ln_b"][l, 2][None])
    return x2.reshape(bsz, n, D_MODEL)


def kernel(x_prompt, x_sample, ffn1_w_in, ffn1_w_out, ffn2_w_in, ffn2_w_out, ln_g, ln_b, ev_w_in, ev_gate_b, ev_conv_w, ev_gla_a2_w, ev_gla_a2_b, ev_norm_w, ev_w_out, od_w_in, od_lb_logits, od_norm_w, od_w_out):
    n_even, n_odd = ev_w_in.shape[0], od_w_in.shape[0]
    p = {
        "ffn1_in": ffn1_w_in.astype(BF16), "ffn1_out": ffn1_w_out.astype(BF16),
        "ffn2_in": ffn2_w_in.astype(BF16), "ffn2_out": ffn2_w_out.astype(BF16),
        "ln_g": ln_g, "ln_b": ln_b,
        "even": [_even_weights(ev_w_in[j], ev_gate_b[j], ev_conv_w[j], ev_gla_a2_w[j], ev_gla_a2_b[j])
                 for j in range(n_even)],
        "ev_nw": [ev_norm_w[j][None, :] for j in range(n_even)],
        "ev_out": ev_w_out.astype(BF16),
        "od_in": od_w_in.astype(BF16),
        "od_lb_logits": od_lb_logits,
        "od_nw": [jnp.concatenate([od_norm_w[j], jnp.ones((MIX_W,), F32)])[None, :] for j in range(n_odd)],
        "od_out": od_w_out.astype(BF16),
    }
    return (_trunk(x_prompt, p), _trunk(x_sample, p))
```

```python
import functools
import math

import numpy as np
import jax
import jax.numpy as jnp
from jax import lax
from jax.experimental import pallas as pl
from jax.experimental.pallas import tpu as pltpu

F32 = jnp.float32
BF16 = jnp.bfloat16

D_MODEL = 1024
DEPTH = 4
D_FF = 2816
LN_EPS = 1e-5
ALPHA = (2 * DEPTH) ** 0.25
NEG_BIG = -1e30
TINY = 1e-30
LOG2E = math.log2(math.e)

HEADS = 4
DH = 128
MIX_W = HEADS * DH
A_W = MIX_W
N_GATES = 4 * HEADS
CONV_W = 5
B_DK = 64
B_KW = HEADS * B_DK
B_RANK = 16
GLA_GATE_NORMALIZER = 16.0
ROPE_BASE = 10000.0

LANES = 128
SUBLANES = 8
VMEM_LIMIT = 56 * 1024 * 1024

FFN_TM = 512
MIX_FFN_TM = 512
FFN_FC = 256
PROJ_TM = 512
SCALAR_CHUNK = 256
VEC_CHUNK = 128
VEC_SUB = SUBLANES


def _dot(a, b):
    return jnp.dot(a, b, preferred_element_type=F32)


def _dot_nt(a, b):
    return lax.dot_general(a, b, (((1,), (1,)), ((), ())), preferred_element_type=F32)


def _dot_tn(a, b):
    return lax.dot_general(a, b, (((0,), (0,)), ((), ())), preferred_element_type=F32)


def _dot_exact(w_bf16, x, terms=3):
    hi = x.astype(BF16)
    r1 = x - hi.astype(F32)
    mid = r1.astype(BF16)
    out = _dot(w_bf16, hi) + _dot(w_bf16, mid)
    if terms == 3:
        out = out + _dot(w_bf16, (r1 - mid.astype(F32)).astype(BF16))
    return out


def _sigmoid(x):
    return 1.0 / (1.0 + jnp.exp(-x))


def _silu(x):
    return x * _sigmoid(x)


def _log_sigmoid(x):
    return jnp.minimum(x, 0.0) - jnp.log1p(jnp.exp(-jnp.abs(x)))


def _layer_norm(y, g, b):
    mu = jnp.mean(y, -1, keepdims=True)
    d = y - mu
    var = jnp.mean(d * d, -1, keepdims=True)
    return d * lax.rsqrt(var + LN_EPS) * g + b


def _head_ln(x):
    mu = jnp.mean(x, -1, keepdims=True)
    d = x - mu
    var = jnp.mean(d * d, -1, keepdims=True)
    return d * lax.rsqrt(var + LN_EPS)


def _head_rms(x):
    return x * lax.rsqrt(jnp.mean(x * x, -1, keepdims=True) + LN_EPS)


def _params(n_grid):
    return pltpu.CompilerParams(dimension_semantics=("arbitrary",) * n_grid,
                                vmem_limit_bytes=VMEM_LIMIT)


def _full_spec(shape):
    nd = len(shape)
    return pl.BlockSpec(shape, lambda *_: (0,) * nd)


def _const_spec(shape):
    nd = len(shape)
    return pl.BlockSpec(shape, lambda *_: (0,) * nd, pipeline_mode=pl.Buffered(1))


def _lane_tile(x, width):
    return jnp.concatenate([x] * (width // x.shape[-1]), axis=-1)


def _ffn_block(x, win_ref, wout_ref, g, b):
    xb = x.astype(BF16)
    acc = jnp.zeros(x.shape, F32)
    for j in range(D_FF // FFN_FC):
        lo = j * FFN_FC
        gate = _dot(xb, win_ref[:, lo:lo + FFN_FC])
        up = _dot(xb, win_ref[:, D_FF + lo:D_FF + lo + FFN_FC])
        h = (_silu(gate) * up).astype(BF16)
        acc = acc + _dot(h, wout_ref[lo:lo + FFN_FC, :])
    return _layer_norm(ALPHA * x + 0.5 * acc, g, b)


def _ffn_kernel(x_ref, win_ref, wout_ref, g_ref, b_ref, o_ref):
    o_ref[...] = _ffn_block(x_ref[...], win_ref, wout_ref, g_ref[...], b_ref[...])


def _ffn_ln(x2d, w_in, w_out, g, b):
    t = x2d.shape[0]
    tm = min(FFN_TM, t)
    return pl.pallas_call(
        _ffn_kernel,
        grid=(t // tm,),
        in_specs=[pl.BlockSpec((tm, D_MODEL), lambda i: (i, 0)),
                  _const_spec(w_in.shape), _const_spec(w_out.shape),
                  _const_spec(g.shape), _const_spec(b.shape)],
        out_specs=pl.BlockSpec((tm, D_MODEL), lambda i: (i, 0)),
        out_shape=jax.ShapeDtypeStruct((t, D_MODEL), F32),
        compiler_params=_params(1),
        name="ffn_ln",
    )(x2d, w_in, w_out, g, b)


_EV_OFF = {"av": 0, "ao": 512, "bq": 1024, "bk": 1280, "bv": 1536, "bg": 2048, "ag": 2560, "ba": 2688}
HALO = SUBLANES


def _even_in_kernel(xp_ref, xc_ref, xn_ref, wqk_ref, wrest_ref, convw_ref, gateb_ref, a2w_ref, a2b_ref,
                    qa_ref, ka_ref, va_ref, oga_ref, gates_ref, qb_ref, kb_ref, vb_ref, gb_ref,
                    logaf_ref, logab_ref, ybuf):
    i = pl.program_id(1)
    last = pl.num_programs(1) - 1
    tm = xc_ref.shape[1]
    xc = xc_ref[0].astype(BF16)
    xp = jnp.where(i > 0, xp_ref[0], 0.0).astype(BF16)
    xn = jnp.where(i < last, xn_ref[0], 0.0).astype(BF16)
    wqk = wqk_ref[...]
    ybuf[0:HALO, :] = _dot(xp, wqk)
    ybuf[HALO:HALO + tm, :] = _dot(xc, wqk)
    ybuf[HALO + tm:2 * HALO + tm, :] = _dot(xn, wqk)
    base = HALO - CONV_W // 2
    acc = ybuf[pl.ds(base, tm), :] * convw_ref[0:1, :]
    for j in range(1, CONV_W):
        acc = acc + ybuf[pl.ds(base + j, tm), :] * convw_ref[j:j + 1, :]
    qk = _silu(acc)
    qa_ref[0] = qk[:, :A_W].astype(BF16)
    ka_ref[0] = (qk[:, A_W:] * DH ** -0.5).astype(BF16)

    def sect(name, width):
        lo = _EV_OFF[name]
        return _dot(xc, wrest_ref[:, lo:lo + width])

    va_ref[0] = sect("av", MIX_W).astype(BF16)
    oga_ref[0] = _sigmoid(sect("ao", MIX_W)).astype(BF16)
    qb_ref[0] = (sect("bq", B_KW) * B_DK ** -0.5).astype(BF16)
    kb_ref[0] = sect("bk", B_KW).astype(BF16)
    vb_ref[0] = sect("bv", MIX_W).astype(BF16)
    gb_ref[0] = _silu(sect("bg", MIX_W)).astype(BF16)
    gates = sect("ag", LANES) + gateb_ref[...]
    col = lax.broadcasted_iota(jnp.int32, gates.shape, 1)
    is_f = (col % 8) >= 4
    gates_ref[0] = jnp.where(col < 2 * N_GATES, jnp.where(is_f, _log_sigmoid(gates), gates), 0.0)
    ba = sect("ba", LANES).astype(BF16)
    a_pre = _dot(ba, a2w_ref[...]) + a2b_ref[...]
    log_a = _log_sigmoid(a_pre) * (1.0 / GLA_GATE_NORMALIZER)
    logaf_ref[0] = log_a[:, :B_KW]
    logab_ref[0] = log_a[:, B_KW:]


def _even_in(x, wqk, wrest, convw, gateb, a2w, a2b):
    bsz, n, _ = x.shape
    tm = min(PROJ_TM, n)
    nt = n // tm
    r = tm // HALO
    nblk8 = n // HALO

    def tile(w, dt):
        return (pl.BlockSpec((1, tm, w), lambda b, i: (b, i, 0)), jax.ShapeDtypeStruct((bsz, n, w), dt))

    outs = [tile(A_W, BF16), tile(A_W, BF16), tile(MIX_W, BF16), tile(MIX_W, BF16), tile(LANES, F32),
            tile(B_KW, BF16), tile(B_KW, BF16), tile(MIX_W, BF16), tile(MIX_W, BF16),
            tile(B_KW, F32), tile(B_KW, F32)]
    return pl.pallas_call(
        _even_in_kernel,
        grid=(bsz, nt),
        in_specs=[pl.BlockSpec((1, HALO, D_MODEL), lambda b, i: (b, jnp.maximum(i * r - 1, 0), 0)),
                  pl.BlockSpec((1, tm, D_MODEL), lambda b, i: (b, i, 0)),
                  pl.BlockSpec((1, HALO, D_MODEL), lambda b, i: (b, jnp.minimum((i + 1) * r, nblk8 - 1), 0)),
                  _full_spec(wqk.shape), _full_spec(wrest.shape), _full_spec(convw.shape),
                  _full_spec(gateb.shape), _full_spec(a2w.shape), _full_spec(a2b.shape)],
        out_specs=[o[0] for o in outs],
        out_shape=[o[1] for o in outs],
        scratch_shapes=[pltpu.VMEM((tm + 2 * HALO, 2 * A_W), F32)],
        compiler_params=_params(2),
        name="even_in",
    )(x, x, x, wqk, wrest, convw, gateb, a2w, a2b)


def _odd_in_kernel(x_ref, w_ref, lb_ref, cos_ref, sin_ref,
                   qc_ref, lff_ref, lfb_ref, kcf_ref, kcb_ref, vc_ref, gc_ref,
                   qr_ref, kr_ref, vr_ref, gr_ref):
    xb = x_ref[0].astype(BF16)

    def sect(idx):
        return _dot(xb, w_ref[:, idx * MIX_W:(idx + 1) * MIX_W])

    qc_ref[0] = (_silu(sect(0)) * DH ** -0.5).astype(BF16)
    for d, (lf_ref, k_ref) in enumerate(((lff_ref, kcf_ref), (lfb_ref, kcb_ref))):
        z = sect(1 + d)
        lb = lb_ref[d:d + 1, :]
        f = lb + (1.0 - lb) * _sigmoid(z)
        lf_ref[0] = jnp.log(jnp.maximum(f, TINY))
        k_ref[0] = ((1.0 - lb) * _sigmoid(-z)).astype(BF16)
    vc_ref[0] = sect(3).astype(BF16)
    gc_ref[0] = _silu(sect(4)).astype(BF16)
    cos2 = cos_ref[...]
    sin2 = sin_ref[...]

    def rotary(y, scale):
        parts = []
        for h in range(HEADS):
            yh = y[:, h * DH:(h + 1) * DH]
            parts.append((yh * cos2 + pltpu.roll(yh, DH // 2, axis=1) * sin2) * scale)
        return jnp.concatenate(parts, axis=-1).astype(BF16)

    qr_ref[0] = rotary(sect(5), 1.0)
    kr_ref[0] = rotary(sect(6), DH ** -0.5)
    vr_ref[0] = sect(7).astype(BF16)
    gr_ref[0] = _silu(sect(8)).astype(BF16)


def _odd_in(x, w, lb, cos2, sin2):
    bsz, n, _ = x.shape
    tm = min(PROJ_TM, n)

    def tile(dt):
        return (pl.BlockSpec((1, tm, MIX_W), lambda b, i: (b, i, 0)), jax.ShapeDtypeStruct((bsz, n, MIX_W), dt))

    dts = [BF16, F32, F32, BF16, BF16, BF16, BF16, BF16, BF16, BF16, BF16]
    outs = [tile(dt) for dt in dts]
    return pl.pallas_call(
        _odd_in_kernel,
        grid=(bsz, n // tm),
        in_specs=[pl.BlockSpec((1, tm, D_MODEL), lambda b, i: (b, i, 0)),
                  _full_spec(w.shape), _full_spec(lb.shape),
                  pl.BlockSpec((tm, DH), lambda b, i: (i, 0)),
                  pl.BlockSpec((tm, DH), lambda b, i: (i, 0))],
        out_specs=[o[0] for o in outs],
        out_shape=[o[1] for o in outs],
        compiler_params=_params(2),
        name="odd_in",
    )(x, w, lb, cos2, sin2)


def _hgrn_lower_bound(lb_logits, layer_idx):
    n_odd = lb_logits.shape[0]

    def kern(l_ref, o_ref):
        rows = [l_ref[i] for i in range(n_odd)]
        mx = rows[0]
        for r in rows[1:]:
            mx = jnp.maximum(mx, r)
        es = [jnp.exp(r - mx) for r in rows]
        tot = es[0]
        for e in es[1:]:
            tot = tot + e
        acc = jnp.zeros_like(tot)
        for i in range(1, layer_idx + 1):
            acc = acc + es[i] / tot
        o_ref[...] = acc

    return pl.pallas_call(
        kern,
        out_shape=jax.ShapeDtypeStruct(lb_logits.shape[1:], F32),
        name="hgrn_lb",
    )(lb_logits)


def _mlstm_kernel(cum_ref, sel_ref, qf_ref, kf_ref, vf_ref, gf_ref, qb_ref, kb_ref, vb_ref, gb_ref,
                  of_ref, ob_ref, c_ref, m_ref):
    c = qf_ref.shape[1]

    @pl.when(pl.program_id(1) == 0)
    def _():
        c_ref[...] = jnp.zeros_like(c_ref)
        m_ref[...] = jnp.zeros_like(m_ref)

    row = lax.broadcasted_iota(jnp.int32, (c, c), 0)
    colm = lax.broadcasted_iota(jnp.int32, (c, c), 1)
    ones_blk = jnp.ones((c, DH), BF16)
    dirs = ((qf_ref, kf_ref, vf_ref, gf_ref, of_ref, colm <= row, c - 1),
            (qb_ref, kb_ref, vb_ref, gb_ref, ob_ref, colm >= row, 0))
    work = []
    for d, (q_ref, k_ref, v_ref, g_ref, o_ref, mask, end) in enumerate(dirs):
        gates = g_ref[0] * LOG2E
        cum = _dot_exact(cum_ref[d], gates)
        gate_lane = lax.broadcasted_iota(jnp.int32, (1, LANES), 1)
        comb = jnp.where(gate_lane < N_GATES, gates, cum)
        comb_hi = comb.astype(BF16)
        xs = jnp.concatenate([comb_hi, (comb - comb_hi.astype(F32)).astype(BF16)], axis=-1)
        gates_t = gates.T
        cum_t = cum.T
        for h in range(HEADS):
            ci, cf = 8 * d + h, 8 * d + 4 + h
            sl = slice(h * DH, (h + 1) * DH)
            s = d * HEADS + h
            q = q_ref[0, :, sl]
            k = k_ref[0, :, sl]
            v_ext = jnp.concatenate([v_ref[0, :, sl], ones_blk], axis=-1)
            rep = _dot(xs, sel_ref[s])
            c_old = c_ref[s]
            work.append(dict(
                s=s, sl=sl, o_ref=o_ref, mask=mask, k=k, v_ext=v_ext, c_old=c_old,
                i_rep=rep[:, :DH], b_rep=rep[:, DH:],
                r_row=gates_t[ci:ci + 1, :] - cum_t[cf:cf + 1, :],
                tot=rep[end:end + 1, DH:],
                m_old=m_ref[s:s + 1, :],
                scores=_dot_nt(q, k), inter=_dot(q, c_old.astype(BF16))))
    for w in work:
        w["dmat"] = jnp.where(w["mask"], _lane_tile(w["b_rep"], c) + w["r_row"], NEG_BIG)
        w["m_inter"] = w["b_rep"] + w["m_old"]
        w["m_t"] = jnp.maximum(w["m_inter"], jnp.max(w["dmat"], axis=-1, keepdims=True))
    for w in work:
        w["att"] = (w["scores"] * jnp.exp2(w["dmat"] - _lane_tile(w["m_t"], c))).astype(BF16)
        w["sc"] = jnp.exp2(w["m_inter"] - w["m_t"])
    for w in work:
        ext = _lane_tile(w["sc"], 2 * DH) * w["inter"] + _dot(w["att"], w["v_ext"])
        num, den = ext[:, :DH], ext[:, DH:]
        w["o_ref"][0, :, w["sl"]] = (num / jnp.maximum(jnp.abs(den), jnp.exp2(-w["m_t"]))).astype(BF16)
    for w in work:
        s = w["s"]
        g_end = w["tot"] + w["i_rep"] - w["b_rep"]
        m_new = jnp.maximum(w["tot"] + w["m_old"], jnp.max(g_end, axis=0, keepdims=True))
        decay = jnp.exp2(w["tot"] + w["m_old"] - m_new)
        wk = (w["k"].astype(F32) * jnp.exp2(g_end - m_new)).astype(BF16)
        c_ref[s] = _lane_tile(decay, 2 * DH) * w["c_old"] + _dot_tn(wk, w["v_ext"])
        m_ref[s:s + 1, :] = m_new


def _mlstm_consts(c):
    low = np.tril(np.ones((c, c), np.float32))
    cum = np.stack([low, low.T])
    sel = np.zeros((2 * HEADS, 2 * LANES, 2 * DH), np.float32)
    for d in range(2):
        for h in range(HEADS):
            for part in range(2):
                sel[d * HEADS + h, part * LANES + 8 * d + h, :DH] = 1.0
                sel[d * HEADS + h, part * LANES + N_GATES + 8 * d + 4 + h, DH:] = 1.0
    return jnp.asarray(cum, BF16), jnp.asarray(sel, BF16)


def _mlstm(q, k, v, gates):
    bsz, n, _ = q.shape
    c = min(SCALAR_CHUNK, n)
    nc = n // c

    def fwd(w):
        return pl.BlockSpec((1, c, w), lambda b, i: (b, i, 0))

    def bwd(w):
        return pl.BlockSpec((1, c, w), lambda b, i: (b, nc - 1 - i, 0))

    cum, sel = _mlstm_consts(c)
    out = jax.ShapeDtypeStruct((bsz, n, MIX_W), BF16)
    return pl.pallas_call(
        _mlstm_kernel,
        grid=(bsz, nc),
        in_specs=[_full_spec(cum.shape), _full_spec(sel.shape),
                  fwd(MIX_W), fwd(MIX_W), fwd(MIX_W), fwd(LANES),
                  bwd(MIX_W), bwd(MIX_W), bwd(MIX_W), bwd(LANES)],
        out_specs=[fwd(MIX_W), bwd(MIX_W)],
        out_shape=[out, out],
        scratch_shapes=[pltpu.VMEM((2 * HEADS, DH, 2 * DH), F32),
                        pltpu.VMEM((2 * HEADS, LANES), F32)],
        compiler_params=_params(2),
        name="mlstm_scan",
    )(cum, sel, q, k, v, gates, q, k, v, gates)


def _vec_levels(c):
    lv, h = [], VEC_SUB
    while h < c:
        lv.append(h)
        h *= 2
    return lv


def _vec_consts(c, hpg):
    levels = _vec_levels(c)
    t = np.arange(c)
    low = np.tril(np.ones((c, c), np.float32))
    lev = np.full((c, c), -2, np.int32)
    same_sub = (t[:, None] // VEC_SUB) == (t[None, :] // VEC_SUB)
    lev[same_sub & (t[None, :] <= t[:, None])] = -1
    for li, h in enumerate(levels):
        blk = t // (2 * h)
        second = (t // h) % 2 == 1
        pair = (blk[:, None] == blk[None, :]) & second[:, None] & (~second)[None, :]
        lev[pair] = li
    wall = jnp.asarray(np.stack([low, low.T]), BF16)
    levid = np.stack([np.tile(lev, (1, hpg)), np.tile(lev[::-1, ::-1], (1, hpg))])
    kmask = np.zeros((hpg * c, VEC_SUB * LANES), np.float32)
    lane_head = np.arange(LANES) // (LANES // hpg)
    for hh in range(hpg):
        for j in range(VEC_SUB):
            rows = hh * c + t[t % VEC_SUB == j]
            kmask[np.ix_(rows, j * LANES + np.nonzero(lane_head == hh)[0])] = 1.0
    return wall, jnp.asarray(levid, jnp.int32), jnp.asarray(kmask, BF16), tuple(levels)


def _level_exponent(b, h, reverse):
    pieces = []
    for r0 in range(0, b.shape[0], 2 * h):
        ref = r0 + h if reverse else r0 + h - 1
        bm = b[ref:ref + 1, :]
        first, second = b[r0:r0 + h], b[r0 + h:r0 + 2 * h]
        pieces += [first - bm, bm - second] if reverse else [bm - first, second - bm]
    return jnp.concatenate(pieces, axis=0)


def _vec_kernel(hpg, levels, wall_ref, lev_ref, kmask_ref, q_f, k_f, v_f, g_f, q_b, k_b, v_b, g_b,
                of_ref, ob_ref, st_ref):
    c = q_f.shape[1]
    ngrp = q_f.shape[2] // LANES
    sub = VEC_SUB

    @pl.when(pl.program_id(1) == 0)
    def _():
        st_ref[...] = jnp.zeros_like(st_ref)

    lane = lax.broadcasted_iota(jnp.int32, (1, LANES), 1)
    head_of_lane = lane // (LANES // hpg)

    def hsel(x, hh):
        return x if hpg == 1 else jnp.where(head_of_lane == hh, x, 0.0)

    def heads_stacked(x):
        return jnp.concatenate([hsel(x, hh) for hh in range(hpg)], axis=0).astype(BF16)

    kmask = kmask_ref[...]
    dirs = ((q_f, k_f, v_f, g_f, of_ref, c - 1), (q_b, k_b, v_b, g_b, ob_ref, 0))
    work = []
    for d, (q_ref, k_ref, v_ref, g_ref, o_ref, end) in enumerate(dirs):
        rev = d == 1
        b_all = _dot_exact(wall_ref[d], g_ref[0] * LOG2E, terms=2)
        for grp in range(ngrp):
            ls = slice(grp * LANES, (grp + 1) * LANES)
            q = q_ref[0, :, ls].astype(F32)
            kb = k_ref[0, :, ls]
            k = kb.astype(F32)
            b = b_all[:, ls]
            b3 = b.reshape(c // sub, sub, LANES)
            parts = []
            for j in range(sub):
                bj = jnp.broadcast_to(b3[:, j:j + 1, :], b3.shape).reshape(c, LANES)
                parts.append((q * jnp.exp2(jnp.minimum(b - bj, 0.0))).astype(BF16))
            zs = []
            for h in levels:
                qk = jnp.concatenate(
                    [x[r0 + i * h:r0 + (i + 1) * h] for r0 in range(0, c, 2 * h)
                     for i, x in enumerate((q, k) if rev else (k, q))], axis=0)
                zs.append(qk * jnp.exp2(_level_exponent(b, h, rev)))
            b_end = b[end:end + 1, :]
            work.append(dict(
                d=d, rev=rev, grp=grp, s=d * ngrp + grp, v_ref=v_ref, o_ref=o_ref, zs=zs,
                qe_cat=jnp.concatenate(parts, axis=-1),
                kspread=jnp.tile(kb, (hpg, sub)) * kmask,
                qe0=q * jnp.exp2(b), kw=k * jnp.exp2(b_end - b), decay=jnp.exp2(b_end)))
    for w in work:
        lev = lev_ref[w["d"]]
        a = jnp.where(lev == -1, _dot_nt(w["qe_cat"], w["kspread"]), 0.0)
        for li, (h, z) in enumerate(zip(levels, w["zs"])):
            q_rows = [(r0, r0 + h) if w["rev"] else (r0 + h, r0 + 2 * h) for r0 in range(0, c, 2 * h)]
            lhs = jnp.concatenate([z[lo:hi] for lo, hi in q_rows], axis=0).astype(BF16)
            sc = _dot_nt(lhs, heads_stacked(z))
            slabs = [a[r:r + h] for r in range(0, c, h)]
            for bi, (lo, hi) in enumerate(q_rows):
                slabs[lo // h] = jnp.where(lev[lo:hi] == li, sc[bi * h:(bi + 1) * h], slabs[lo // h])
            a = jnp.concatenate(slabs, axis=0)
        w["a"] = a.astype(BF16)
    for w in work:
        s = w["s"]
        st = st_ref[s]
        stb = st.astype(BF16)
        upd = None
        for hh in range(hpg):
            head = w["grp"] * hpg + hh
            vs = slice(head * DH, (head + 1) * DH)
            v = w["v_ref"][0, :, vs]
            w["o_ref"][0, :, vs] = (_dot_nt(hsel(w["qe0"], hh).astype(BF16), stb)
                                    + _dot(w["a"][:, hh * c:(hh + 1) * c], v)).astype(BF16)
            u = _dot_tn(v, hsel(w["kw"], hh).astype(BF16))
            upd = u if upd is None else upd + u
        st_ref[s] = st * w["decay"] + upd


def _vec_scan(q, k_f, k_b, v, g_f, g_b, hpg, name):
    bsz, n, kw = q.shape
    c = min(VEC_CHUNK, n)
    nc = n // c
    wall, levid, kmask, levels = _vec_consts(c, hpg)

    def fwd(w):
        return pl.BlockSpec((1, c, w), lambda b, i: (b, i, 0))

    def bwd(w):
        return pl.BlockSpec((1, c, w), lambda b, i: (b, nc - 1 - i, 0))

    out = jax.ShapeDtypeStruct((bsz, n, MIX_W), BF16)
    return pl.pallas_call(
        functools.partial(_vec_kernel, hpg, levels),
        grid=(bsz, nc),
        in_specs=[_full_spec(wall.shape), _full_spec(levid.shape), _full_spec(kmask.shape),
                  fwd(kw), fwd(kw), fwd(MIX_W), fwd(kw),
                  bwd(kw), bwd(kw), bwd(MIX_W), bwd(kw)],
        out_specs=[fwd(MIX_W), bwd(MIX_W)],
        out_shape=[out, out],
        scratch_shapes=[pltpu.VMEM((2 * kw // LANES, DH, LANES), F32)],
        compiler_params=_params(2),
        name=name,
    )(wall, levid, kmask, q, k_f, v, g_f, q, k_b, v, g_b)


def _ret_consts(c):
    log_gamma = np.log1p(-(2.0 ** (-5.0 - np.arange(HEADS, dtype=np.float64))))
    t = np.arange(c, dtype=np.float64)
    diff = t[:, None] - t[None, :]
    dmat = np.zeros((2, HEADS, c, c))
    qdec = np.zeros((2, c, MIX_W))
    kdec = np.zeros((2, c, MIX_W))
    for h in range(HEADS):
        lg = log_gamma[h]
        dmat[0, h] = np.where(diff >= 0, np.exp(lg * np.maximum(diff, 0)), 0.0)
        dmat[1, h] = dmat[0, h].T
        sl = slice(h * DH, (h + 1) * DH)
        qdec[0, :, sl] = np.exp(lg * (t + 1))[:, None]
        qdec[1, :, sl] = np.exp(lg * (c - t))[:, None]
        kdec[0, :, sl] = np.exp(lg * (c - 1 - t))[:, None]
        kdec[1, :, sl] = np.exp(lg * t)[:, None]
    sdec = [float(np.exp(lg * c)) for lg in log_gamma]
    return (jnp.asarray(dmat, F32), jnp.asarray(qdec, F32), jnp.asarray(kdec, F32)), sdec


def _ret_kernel(sdec, dmat_ref, qdec_ref, kdec_ref, q_f, k_f, v_f, q_b, k_b, v_b, of_ref, ob_ref, st_ref):
    @pl.when(pl.program_id(1) == 0)
    def _():
        st_ref[...] = jnp.zeros_like(st_ref)

    dirs = ((q_f, k_f, v_f, of_ref), (q_b, k_b, v_b, ob_ref))
    work = []
    for d, (q_ref, k_ref, v_ref, o_ref) in enumerate(dirs):
        for h in range(HEADS):
            sl = slice(h * DH, (h + 1) * DH)
            s = d * HEADS + h
            q = q_ref[0, :, sl]
            k = k_ref[0, :, sl]
            st = st_ref[s]
            qd = (q.astype(F32) * qdec_ref[d, :, sl]).astype(BF16)
            work.append(dict(d=d, h=h, s=s, sl=sl, o_ref=o_ref, v=v_ref[0, :, sl], st=st,
                             scores=_dot_nt(q, k), inter=_dot(qd, st.astype(BF16)),
                             kd=(k.astype(F32) * kdec_ref[d, :, sl]).astype(BF16)))
    for w in work:
        w["att"] = (w["scores"] * dmat_ref[w["d"], w["h"]]).astype(BF16)
    for w in work:
        w["o_ref"][0, :, w["sl"]] = (w["inter"] + _dot(w["att"], w["v"])).astype(BF16)
    for w in work:
        st_ref[w["s"]] = sdec[w["h"]] * w["st"] + _dot_tn(w["kd"], w["v"])


def _retention(q, k, v):
    bsz, n, _ = q.shape
    c = min(SCALAR_CHUNK, n)
    nc = n // c
    consts, sdec = _ret_consts(c)

    def fwd(w):
        return pl.BlockSpec((1, c, w), lambda b, i: (b, i, 0))

    def bwd(w):
        return pl.BlockSpec((1, c, w), lambda b, i: (b, nc - 1 - i, 0))

    out = jax.ShapeDtypeStruct((bsz, n, MIX_W), BF16)
    return pl.pallas_call(
        functools.partial(_ret_kernel, sdec),
        grid=(bsz, nc),
        in_specs=[_full_spec(a.shape) for a in consts] + [fwd(MIX_W)] * 3 + [bwd(MIX_W)] * 3,
        out_specs=[fwd(MIX_W), bwd(MIX_W)],
        out_shape=[out, out],
        scratch_shapes=[pltpu.VMEM((2 * HEADS, DH, DH), F32)],
        compiler_params=_params(2),
        name="retention_scan",
    )(*consts, q, k, v, q, k, v)


def _mix_ffn_kernel(norms, h1f_ref, h1b_ref, g1_ref, h2f_ref, h2b_ref, g2_ref, x_ref,
                    nw_ref, wmix_ref, lg1_ref, lb1_ref, win_ref, wout_ref, lg2_ref, lb2_ref, o_ref):
    parts = []
    groups = ((h1f_ref, h1b_ref, g1_ref, norms[0], 0), (h2f_ref, h2b_ref, g2_ref, norms[1], MIX_W))
    for hf_ref, hb_ref, gate_ref, norm, off in groups:
        hsum = hf_ref[...].astype(F32) + hb_ref[...].astype(F32)
        gate = gate_ref[...].astype(F32)
        for h in range(HEADS):
            sl = slice(h * DH, (h + 1) * DH)
            y = norm(hsum[:, sl]) * nw_ref[:, off + h * DH:off + (h + 1) * DH] * gate[:, sl]
            parts.append(y.astype(BF16))
    mix = _dot(jnp.concatenate(parts, axis=-1), wmix_ref[...])
    x1 = _layer_norm(ALPHA * x_ref[...] + mix, lg1_ref[...], lb1_ref[...])
    o_ref[...] = _ffn_block(x1, win_ref, wout_ref, lg2_ref[...], lb2_ref[...])


def _mix_ffn(norms, h1f, h1b, g1, h2f, h2b, g2, x2d, nw, w_mix, lg1, lb1, w_in, w_out, lg2, lb2):
    t = x2d.shape[0]
    tm = min(MIX_FFN_TM, t)

    def tile(w):
        return pl.BlockSpec((tm, w), lambda i: (i, 0))

    flat =[a.reshape(t, MIX_W) for a in (h1f, h1b, g1, h2f, h2b, g2)]
    consts = (nw, w_mix, lg1, lb1, w_in, w_out, lg2, lb2)
    return pl.pallas_call(
        functools.partial(_mix_ffn_kernel, norms),
        grid=(t // tm,),
        in_specs=[tile(MIX_W)] * 6 + [tile(D_MODEL)] + [_const_spec(a.shape) for a in consts],
        out_specs=tile(D_MODEL),
        out_shape=jax.ShapeDtypeStruct((t, D_MODEL), F32),
        compiler_params=_params(1),
        name="mix_ffn",
    )(*flat, x2d, *consts)


def _pad_cols(w, width):
    return jnp.pad(w, ((0, 0), (0, width - w.shape[1])))


def _even_weights(w_in, gate_b, conv_w, a2_w, a2_b):
    sizes = (A_W, A_W, A_W, A_W, 4 * HEADS, B_KW, B_KW, MIX_W, MIX_W, 2 * B_RANK)
    offs = np.concatenate([[0], np.cumsum(sizes)])
    aq, ak, av, ao, ag, bq, bk, bv, bg, ba = [w_in[:, offs[i]:offs[i + 1]] for i in range(len(sizes))]
    wqk = jnp.concatenate([aq, ak], axis=1).astype(BF16)
    ag2 = _pad_cols(jnp.concatenate([ag, ag], axis=1), LANES)
    wrest = jnp.concatenate([av, ao, bq, bk, bv, bg, ag2, _pad_cols(ba, LANES)], axis=1).astype(BF16)
    convw = jnp.pad(conv_w, ((0, SUBLANES - CONV_W), (0, 0)))
    gateb = _pad_cols(jnp.concatenate([gate_b, gate_b])[None, :], LANES)
    a2w = jnp.zeros((LANES, 2 * B_KW), F32)
    a2w = a2w.at[0:B_RANK, 0:B_KW].set(a2_w[0]).at[B_RANK:2 * B_RANK, B_KW:].set(a2_w[1]).astype(BF16)
    a2b = jnp.concatenate([a2_b[0], a2_b[1]])[None, :]
    return wqk, wrest, convw, gateb, a2w, a2b


def _rotary_tables(n):
    inv = 1.0 / (ROPE_BASE ** (jnp.arange(0, DH, 2, dtype=F32) / DH))
    ang = jnp.arange(n, dtype=F32)[:, None] * inv[None, :]
    cos, sin = jnp.cos(ang), jnp.sin(ang)
    return jnp.concatenate([cos, cos], -1), jnp.concatenate([-sin, sin], -1)


def _trunk(x, p):
    bsz, n, _ = x.shape
    t = bsz * n
    x2 = x.reshape(t, D_MODEL)
    cos2, sin2 = _rotary_tables(n)
    for l in range(DEPTH):
        j = l // 2
        x2 = _ffn_ln(x2, p["ffn1_in"][l], p["ffn1_out"][l], p["ln_g"][l, 0][None], p["ln_b"][l, 0][None])
        x3 = x2.reshape(bsz, n, D_MODEL)
        if l % 2 == 0:
            qa, ka, va, oga, gates, qb, kb, vb, gb, logaf, logab = _even_in(x3, *p["even"][j])
            h1f, h1b = _mlstm(qa, ka, va, gates)
            h2f, h2b = _vec_scan(qb, kb, kb, vb, logaf, logab, DH // B_DK, "gla_scan")
            norms, g1, g2 = (_head_ln, _head_rms), oga, gb
            nw, w_out = p["ev_nw"][j], p["ev_out"][j]
        else:
            lb = _hgrn_lower_bound(p["od_lb_logits"], j)
            qc, lff, lfb, kcf, kcb, vc, gc, qr, kr, vr, gr = _odd_in(x3, p["od_in"][j], lb, cos2, sin2)
            h1f, h1b = _vec_scan(qc, kcf, kcb, vc, lff, lfb, 1, "hgrn_scan")
            h2f, h2b = _retention(qr, kr, vr)
            norms, g1, g2 = (_head_rms, _head_ln), gc, gr
            nw, w_out = p["od_nw"][j], p["od_out"][j]
        x2 = _mix_ffn(norms, h1f, h1b, g1, h2f, h2b, g2, x2, nw, w_out,
                      p["ln_g"][l, 1][None], p["ln_b"][l, 1][None],
                      p["ffn2_in"][l], p["ffn2_out"][l], p["ln_g"][l, 2][None], p["ln_b"][l, 2][None])
    return x2.reshape(bsz, n, D_MODEL)


def kernel(x_prompt, x_sample, ffn1_w_in, ffn1_w_out, ffn2_w_in, ffn2_w_out, ln_g, ln_b, ev_w_in, ev_gate_b, ev_conv_w, ev_gla_a2_w, ev_gla_a2_b, ev_norm_w, ev_w_out, od_w_in, od_lb_logits, od_norm_w, od_w_out):
    n_even, n_odd = ev_w_in.shape[0], od_w_in.shape[0]
    p = {
        "ffn1_in": ffn1_w_in.astype(BF16), "ffn1_out": ffn1_w_out.astype(BF16),
        "ffn2_in": ffn2_w_in.astype(BF16), "ffn2_out": ffn2_w_out.astype(BF16),
        "ln_g": ln_g, "ln_b": ln_b,
        "even": [_even_weights(ev_w_in[j], ev_gate_b[j], ev_conv_w[j], ev_gla_a2_w[j], ev_gla_a2_b[j])
                 for j in range(n_even)],
        "ev_nw": [ev_norm_w[j][None, :] for j in range(n_even)],
        "ev_out": ev_w_out.astype(BF16),
        "od_in": od_w_in.astype(BF16),
        "od_lb_logits": od_lb_logits,
        "od_nw": [jnp.concatenate([od_norm_w[j], jnp.ones((MIX_W,), F32)])[None, :] for j in range(n_odd)],
        "od_out": od_w_out.astype(BF16),
    }
    return (_trunk(x_prompt, p), _trunk(x_sample, p))
```

```python
import functools
import math

import numpy as np
import jax
import jax.numpy as jnp
from jax import lax
from jax.experimental import pallas as pl
from jax.experimental.pallas import tpu as pltpu

F32 = jnp.float32
BF16 = jnp.bfloat16

D_MODEL = 1024
DEPTH = 4
D_FF = 2816
LN_EPS = 1e-5
ALPHA = (2 * DEPTH) ** 0.25
NEG_BIG = -1e30
TINY = 1e-30
LOG2E = math.log2(math.e)

HEADS = 4
DH = 128
MIX_W = HEADS * DH
A_W = MIX_W
N_GATES = 4 * HEADS
CONV_W = 5
B_DK = 64
B_KW = HEADS * B_DK
B_RANK = 16
GLA_GATE_NORMALIZER = 16.0
ROPE_BASE = 10000.0

LANES = 128
SUBLANES = 8
VMEM_LIMIT = 56 * 1024 * 1024

FFN_TM = 512
MIX_FFN_TM = 512
FFN_FC = 256
PROJ_TM = 512
SCALAR_CHUNK = 256
VEC_CHUNK = 128
VEC_SUB = SUBLANES


def _dot(a, b):
    return jnp.dot(a, b, preferred_element_type=F32)


def _dot_nt(a, b):
    return lax.dot_general(a, b, (((1,), (1,)), ((), ())), preferred_element_type=F32)


def _dot_tn(a, b):
    return lax.dot_general(a, b, (((0,), (0,)), ((), ())), preferred_element_type=F32)


def _dot_exact(w_bf16, x, terms=3):
    hi = x.astype(BF16)
    r1 = x - hi.astype(F32)
    mid = r1.astype(BF16)
    out = _dot(w_bf16, hi) + _dot(w_bf16, mid)
    if terms == 3:
        out = out + _dot(w_bf16, (r1 - mid.astype(F32)).astype(BF16))
    return out


def _sigmoid(x):
    return 1.0 / (1.0 + jnp.exp(-x))


def _silu(x):
    return x * _sigmoid(x)


def _log_sigmoid(x):
    return jnp.minimum(x, 0.0) - jnp.log1p(jnp.exp(-jnp.abs(x)))


def _layer_norm(y, g, b):
    mu = jnp.mean(y, -1, keepdims=True)
    d = y - mu
    var = jnp.mean(d * d, -1, keepdims=True)
    return d * lax.rsqrt(var + LN_EPS) * g + b


def _head_ln(x):
    mu = jnp.mean(x, -1, keepdims=True)
    d = x - mu
    var = jnp.mean(d * d, -1, keepdims=True)
    return d * lax.rsqrt(var + LN_EPS)


def _head_rms(x):
    return x * lax.rsqrt(jnp.mean(x * x, -1, keepdims=True) + LN_EPS)


def _params(n_grid):
    return pltpu.CompilerParams(dimension_semantics=("arbitrary",) * n_grid,
                                vmem_limit_bytes=VMEM_LIMIT)


def _full_spec(shape):
    nd = len(shape)
    return pl.BlockSpec(shape, lambda *_: (0,) * nd)


def _const_spec(shape):
    nd = len(shape)
    return pl.BlockSpec(shape, lambda *_: (0,) * nd, pipeline_mode=pl.Buffered(1))


def _lane_tile(x, width):
    return jnp.concatenate([x] * (width // x.shape[-1]), axis=-1)


def _ffn_block(x, win_ref, wout_ref, g, b):
    xb = x.astype(BF16)
    hs = []
    for j in range(D_FF // FFN_FC):
        lo = j * FFN_FC
        gate = _dot(xb, win_ref[:, lo:lo + FFN_FC])
        up = _dot(xb, win_ref[:, D_FF + lo:D_FF + lo + FFN_FC])
        hs.append((_silu(gate) * up).astype(BF16))
    acc = _dot(jnp.concatenate(hs, axis=-1), wout_ref[...])
    return _layer_norm(ALPHA * x + 0.5 * acc, g, b)


def _ffn_kernel(x_ref, win_ref, wout_ref, g_ref, b_ref, o_ref):
    o_ref[...] = _ffn_block(x_ref[...], win_ref, wout_ref, g_ref[...], b_ref[...])


def _ffn_ln(x2d, w_in, w_out, g, b):
    t = x2d.shape[0]
    tm = min(FFN_TM, t)
    return pl.pallas_call(
        _ffn_kernel,
        grid=(t // tm,),
        in_specs=[pl.BlockSpec((tm, D_MODEL), lambda i: (i, 0)),
                  _const_spec(w_in.shape), _const_spec(w_out.shape),
                  _const_spec(g.shape), _const_spec(b.shape)],
        out_specs=pl.BlockSpec((tm, D_MODEL), lambda i: (i, 0)),
        out_shape=jax.ShapeDtypeStruct((t, D_MODEL), F32),
        compiler_params=_params(1),
        name="ffn_ln",
    )(x2d, w_in, w_out, g, b)


_EV_OFF = {"av": 0, "ao": 512, "bq": 1024, "bk": 1280, "bv": 1536, "bg": 2048, "ag": 2560, "ba": 2688}
HALO = SUBLANES


def _even_in_kernel(xp_ref, xc_ref, xn_ref, wqk_ref, wrest_ref, convw_ref, gateb_ref, a2w_ref, a2b_ref,
                    qa_ref, ka_ref, va_ref, oga_ref, gates_ref, qb_ref, kb_ref, vb_ref, gb_ref,
                    logaf_ref, logab_ref, ybuf):
    i = pl.program_id(1)
    last = pl.num_programs(1) - 1
    tm = xc_ref.shape[1]
    xc = xc_ref[0].astype(BF16)
    xp = jnp.where(i > 0, xp_ref[0], 0.0).astype(BF16)
    xn = jnp.where(i < last, xn_ref[0], 0.0).astype(BF16)
    wqk = wqk_ref[...]
    ybuf[0:HALO, :] = _dot(xp, wqk)
    ybuf[HALO:HALO + tm, :] = _dot(xc, wqk)
    ybuf[HALO + tm:2 * HALO + tm, :] = _dot(xn, wqk)
    base = HALO - CONV_W // 2
    acc = ybuf[pl.ds(base, tm), :] * convw_ref[0:1, :]
    for j in range(1, CONV_W):
        acc = acc + ybuf[pl.ds(base + j, tm), :] * convw_ref[j:j + 1, :]
    qk = _silu(acc)
    qa_ref[0] = qk[:, :A_W].astype(BF16)
    ka_ref[0] = (qk[:, A_W:] * DH ** -0.5).astype(BF16)

    def sect(name, width):
        lo = _EV_OFF[name]
        return _dot(xc, wrest_ref[:, lo:lo + width])

    va_ref[0] = sect("av", MIX_W).astype(BF16)
    oga_ref[0] = _sigmoid(sect("ao", MIX_W)).astype(BF16)
    qb_ref[0] = (sect("bq", B_KW) * B_DK ** -0.5).astype(BF16)
    kb_ref[0] = sect("bk", B_KW).astype(BF16)
    vb_ref[0] = sect("bv", MIX_W).astype(BF16)
    gb_ref[0] = _silu(sect("bg", MIX_W)).astype(BF16)
    gates = sect("ag", LANES) + gateb_ref[...]
    col = lax.broadcasted_iota(jnp.int32, gates.shape, 1)
    is_f = (col % 8) >= 4
    gates_ref[0] = jnp.where(col < 2 * N_GATES, jnp.where(is_f, _log_sigmoid(gates), gates), 0.0)
    ba = sect("ba", LANES).astype(BF16)
    a_pre = _dot(ba, a2w_ref[...]) + a2b_ref[...]
    log_a = _log_sigmoid(a_pre) * (1.0 / GLA_GATE_NORMALIZER)
    logaf_ref[0] = log_a[:, :B_KW]
    logab_ref[0] = log_a[:, B_KW:]


def _even_in(x, wqk, wrest, convw, gateb, a2w, a2b):
    bsz, n, _ = x.shape
    tm = min(PROJ_TM, n)
    nt = n // tm
    r = tm // HALO
    nblk8 = n // HALO

    def tile(w, dt):
        return (pl.BlockSpec((1, tm, w), lambda b, i: (b, i, 0)), jax.ShapeDtypeStruct((bsz, n, w), dt))

    outs = [tile(A_W, BF16), tile(A_W, BF16), tile(MIX_W, BF16), tile(MIX_W, BF16), tile(LANES, F32),
            tile(B_KW, BF16), tile(B_KW, BF16), tile(MIX_W, BF16), tile(MIX_W, BF16),
            tile(B_KW, F32), tile(B_KW, F32)]
    return pl.pallas_call(
        _even_in_kernel,
        grid=(bsz, nt),
        in_specs=[pl.BlockSpec((1, HALO, D_MODEL), lambda b, i: (b, jnp.maximum(i * r - 1, 0), 0)),
                  pl.BlockSpec((1, tm, D_MODEL), lambda b, i: (b, i, 0)),
                  pl.BlockSpec((1, HALO, D_MODEL), lambda b, i: (b, jnp.minimum((i + 1) * r, nblk8 - 1), 0)),
                  _full_spec(wqk.shape), _full_spec(wrest.shape), _full_spec(convw.shape),
                  _full_spec(gateb.shape), _full_spec(a2w.shape), _full_spec(a2b.shape)],
        out_specs=[o[0] for o in outs],
        out_shape=[o[1] for o in outs],
        scratch_shapes=[pltpu.VMEM((tm + 2 * HALO, 2 * A_W), F32)],
        compiler_params=_params(2),
        name="even_in",
    )(x, x, x, wqk, wrest, convw, gateb, a2w, a2b)


def _odd_in_kernel(x_ref, w_ref, lb_ref, cos_ref, sin_ref,
                   qc_ref, lff_ref, lfb_ref, kcf_ref, kcb_ref, vc_ref, gc_ref,
                   qr_ref, kr_ref, vr_ref, gr_ref):
    xb = x_ref[0].astype(BF16)

    def sect(idx):
        return _dot(xb, w_ref[:, idx * MIX_W:(idx + 1) * MIX_W])

    qc_ref[0] = (_silu(sect(0)) * DH ** -0.5).astype(BF16)
    for d, (lf_ref, k_ref) in enumerate(((lff_ref, kcf_ref), (lfb_ref, kcb_ref))):
        z = sect(1 + d)
        lb = lb_ref[d:d + 1, :]
        f = lb + (1.0 - lb) * _sigmoid(z)
        lf_ref[0] = jnp.log(jnp.maximum(f, TINY))
        k_ref[0] = ((1.0 - lb) * _sigmoid(-z)).astype(BF16)
    vc_ref[0] = sect(3).astype(BF16)
    gc_ref[0] = _silu(sect(4)).astype(BF16)
    cos2 = cos_ref[...]
    sin2 = sin_ref[...]

    def rotary(y, scale):
        parts = []
        for h in range(HEADS):
            yh = y[:, h * DH:(h + 1) * DH]
            parts.append((yh * cos2 + pltpu.roll(yh, DH // 2, axis=1) * sin2) * scale)
        return jnp.concatenate(parts, axis=-1).astype(BF16)

    qr_ref[0] = rotary(sect(5), 1.0)
    kr_ref[0] = rotary(sect(6), DH ** -0.5)
    vr_ref[0] = sect(7).astype(BF16)
    gr_ref[0] = _silu(sect(8)).astype(BF16)


def _odd_in(x, w, lb, cos2, sin2):
    bsz, n, _ = x.shape
    tm = min(PROJ_TM, n)

    def tile(dt):
        return (pl.BlockSpec((1, tm, MIX_W), lambda b, i: (b, i, 0)), jax.ShapeDtypeStruct((bsz, n, MIX_W), dt))

    dts = [BF16, F32, F32, BF16, BF16, BF16, BF16, BF16, BF16, BF16, BF16]
    outs = [tile(dt) for dt in dts]
    return pl.pallas_call(
        _odd_in_kernel,
        grid=(bsz, n // tm),
        in_specs=[pl.BlockSpec((1, tm, D_MODEL), lambda b, i: (b, i, 0)),
                  _full_spec(w.shape), _full_spec(lb.shape),
                  pl.BlockSpec((tm, DH), lambda b, i: (i, 0)),
                  pl.BlockSpec((tm, DH), lambda b, i: (i, 0))],
        out_specs=[o[0] for o in outs],
        out_shape=[o[1] for o in outs],
        compiler_params=_params(2),
        name="odd_in",
    )(x, w, lb, cos2, sin2)


def _hgrn_lower_bound(lb_logits, layer_idx):
    n_odd = lb_logits.shape[0]

    def kern(l_ref, o_ref):
        rows = [l_ref[i] for i in range(n_odd)]
        mx = rows[0]
        for r in rows[1:]:
            mx = jnp.maximum(mx, r)
        es = [jnp.exp(r - mx) for r in rows]
        tot = es[0]
        for e in es[1:]:
            tot = tot + e
        acc = jnp.zeros_like(tot)
        for i in range(1, layer_idx + 1):
            acc = acc + es[i] / tot
        o_ref[...] = acc

    return pl.pallas_call(
        kern,
        out_shape=jax.ShapeDtypeStruct(lb_logits.shape[1:], F32),
        name="hgrn_lb",
    )(lb_logits)


def _emit_interleaved(stage_lists):
    for i in range(max(len(st) for st in stage_lists)):
        for st in stage_lists:
            if i < len(st):
                st[i]()


def _scan_specs(n, c):
    nc = n // c

    def fwd(w):
        return pl.BlockSpec((1, c, w), lambda b, i: (b, i, 0))

    def bwd(w):
        return pl.BlockSpec((1, c, w), lambda b, i: (b, nc - 1 - i, 0))

    return nc, fwd, bwd


def _mlstm_stages(cum_ref, sel_ref, data, c_ref, m_ref):
    c = data[0][0].shape[1]
    work = []

    def prep():
        row = lax.broadcasted_iota(jnp.int32, (c, c), 0)
        colm = lax.broadcasted_iota(jnp.int32, (c, c), 1)
        ones_blk = jnp.ones((c, DH), BF16)
        gate_lane = lax.broadcasted_iota(jnp.int32, (1, LANES), 1)
        for d, (q_ref, k_ref, v_ref, g_ref, o_ref) in enumerate(data):
            mask, end = (colm <= row, c - 1) if d == 0 else (colm >= row, 0)
            gates = g_ref[0] * LOG2E
            cum = _dot_exact(cum_ref[d], gates)
            comb = jnp.where(gate_lane < N_GATES, gates, cum)
            comb_hi = comb.astype(BF16)
            xs = jnp.concatenate([comb_hi, (comb - comb_hi.astype(F32)).astype(BF16)], axis=-1)
            gates_t = gates.T
            cum_t = cum.T
            for h in range(HEADS):
                ci, cf = 8 * d + h, 8 * d + 4 + h
                sl = slice(h * DH, (h + 1) * DH)
                s = d * HEADS + h
                q = q_ref[0, :, sl]
                k = k_ref[0, :, sl]
                v_ext = jnp.concatenate([v_ref[0, :, sl], ones_blk], axis=-1)
                rep = _dot(xs, sel_ref[s])
                c_old = c_ref[s]
                work.append(dict(
                    s=s, sl=sl, o_ref=o_ref, mask=mask, k=k, v_ext=v_ext, c_old=c_old,
                    i_rep=rep[:, :DH], b_rep=rep[:, DH:],
                    r_row=gates_t[ci:ci + 1, :] - cum_t[cf:cf + 1, :],
                    tot=rep[end:end + 1, DH:],
                    m_old=m_ref[s:s + 1, :],
                    scores=_dot_nt(q, k), inter=_dot(q, c_old.astype(BF16))))

    def stats():
        for w in work:
            w["dmat"] = jnp.where(w["mask"], _lane_tile(w["b_rep"], c) + w["r_row"], NEG_BIG)
            w["m_inter"] = w["b_rep"] + w["m_old"]
            w["m_t"] = jnp.maximum(w["m_inter"], jnp.max(w["dmat"], axis=-1, keepdims=True))

    def weights():
        for w in work:
            w["att"] = (w["scores"] * jnp.exp2(w["dmat"] - _lane_tile(w["m_t"], c))).astype(BF16)
            w["sc"] = jnp.exp2(w["m_inter"] - w["m_t"])

    def outputs():
        for w in work:
            ext = _lane_tile(w["sc"], 2 * DH) * w["inter"] + _dot(w["att"], w["v_ext"])
            num, den = ext[:, :DH], ext[:, DH:]
            w["o_ref"][0, :, w["sl"]] = (num / jnp.maximum(jnp.abs(den), jnp.exp2(-w["m_t"]))).astype(BF16)

    def update():
        for w in work:
            s = w["s"]
            g_end = w["tot"] + w["i_rep"] - w["b_rep"]
            m_new = jnp.maximum(w["tot"] + w["m_old"], jnp.max(g_end, axis=0, keepdims=True))
            decay = jnp.exp2(w["tot"] + w["m_old"] - m_new)
            wk = (w["k"].astype(F32) * jnp.exp2(g_end - m_new)).astype(BF16)
            c_ref[s] = _lane_tile(decay, 2 * DH) * w["c_old"] + _dot_tn(wk, w["v_ext"])
            m_ref[s:s + 1, :] = m_new

    return [prep, stats, weights, outputs, update]


def _mlstm_consts(c):
    low = np.tril(np.ones((c, c), np.float32))
    cum = np.stack([low, low.T])
    sel = np.zeros((2 * HEADS, 2 * LANES, 2 * DH), np.float32)
    for d in range(2):
        for h in range(HEADS):
            for part in range(2):
                sel[d * HEADS + h, part * LANES + 8 * d + h, :DH] = 1.0
                sel[d * HEADS + h, part * LANES + N_GATES + 8 * d + 4 + h, DH:] = 1.0
    return jnp.asarray(cum, BF16), jnp.asarray(sel, BF16)


def _vec_levels(c):
    lv, h = [], VEC_SUB
    while h < c:
        lv.append(h)
        h *= 2
    return lv


def _vec_consts(c, hpg):
    levels = _vec_levels(c)
    t = np.arange(c)
    low = np.tril(np.ones((c, c), np.float32))
    lev = np.full((c, c), -2, np.int32)
    same_sub = (t[:, None] // VEC_SUB) == (t[None, :] // VEC_SUB)
    lev[same_sub & (t[None, :] <= t[:, None])] = -1
    for li, h in enumerate(levels):
        blk = t // (2 * h)
        second = (t // h) % 2 == 1
        pair = (blk[:, None] == blk[None, :]) & second[:, None] & (~second)[None, :]
        lev[pair] = li
    wall = jnp.asarray(np.stack([low, low.T]), BF16)
    levid = np.stack([np.tile(lev, (1, hpg)), np.tile(lev[::-1, ::-1], (1, hpg))])
    kmask = np.zeros((hpg * c, VEC_SUB * LANES), np.float32)
    lane_head = np.arange(LANES) // (LANES // hpg)
    for hh in range(hpg):
        for j in range(VEC_SUB):
            rows = hh * c + t[t % VEC_SUB == j]
            kmask[np.ix_(rows, j * LANES + np.nonzero(lane_head == hh)[0])] = 1.0
    return wall, jnp.asarray(levid, jnp.int32), jnp.asarray(kmask, BF16), tuple(levels)


def _level_exponent(b, h, reverse):
    pieces = []
    for r0 in range(0, b.shape[0], 2 * h):
        ref = r0 + h if reverse else r0 + h - 1
        bm = b[ref:ref + 1, :]
        first, second = b[r0:r0 + h], b[r0 + h:r0 + 2 * h]
        pieces += [first - bm, bm - second] if reverse else [bm - first, second - bm]
    return jnp.concatenate(pieces, axis=0)


def _vec_stages(hpg, levels, wall_ref, lev_ref, kmask_ref, data, st_ref, row0):
    c = VEC_CHUNK
    ngrp = data[0][0].shape[2] // LANES
    sub = VEC_SUB
    work = []

    def hsel(x, hh):
        if hpg == 1:
            return x
        lane = lax.broadcasted_iota(jnp.int32, (1, LANES), 1)
        return jnp.where(lane // (LANES // hpg) == hh, x, 0.0)

    def heads_stacked(x):
        return jnp.concatenate([hsel(x, hh) for hh in range(hpg)], axis=0).astype(BF16)

    def prep():
        kmask = kmask_ref[...]
        for d, (q_ref, k_ref, v_ref, g_ref, o_ref) in enumerate(data):
            rev = d == 1
            rows = pl.ds(row0[d], c)
            end = 0 if rev else c - 1
            b_all = _dot_exact(wall_ref[d], g_ref[0, rows, :] * LOG2E, terms=2)
            for grp in range(ngrp):
                ls = slice(grp * LANES, (grp + 1) * LANES)
                q = q_ref[0, rows, ls].astype(F32)
                kb = k_ref[0, rows, ls]
                k = kb.astype(F32)
                b = b_all[:, ls]
                b3 = b.reshape(c // sub, sub, LANES)
                parts = []
                for j in range(sub):
                    bj = jnp.broadcast_to(b3[:, j:j + 1, :], b3.shape).reshape(c, LANES)
                    parts.append((q * jnp.exp2(jnp.minimum(b - bj, 0.0))).astype(BF16))
                zs = []
                for h in levels:
                    qk = jnp.concatenate(
                        [x[r0 + i * h:r0 + (i + 1) * h] for r0 in range(0, c, 2 * h)
                         for i, x in enumerate((q, k) if rev else (k, q))], axis=0)
                    zs.append(qk * jnp.exp2(_level_exponent(b, h, rev)))
                b_end = b[end:end + 1, :]
                work.append(dict(
                    d=d, rev=rev, grp=grp, s=d * ngrp + grp, rows=rows, v_ref=v_ref, o_ref=o_ref, zs=zs,
                    qe_cat=jnp.concatenate(parts, axis=-1),
                    kspread=jnp.tile(kb, (hpg, sub)) * kmask,
                    qe0=q * jnp.exp2(b), kw=k * jnp.exp2(b_end - b), decay=jnp.exp2(b_end)))

    def scores():
        for w in work:
            lev = lev_ref[w["d"]]
            a = jnp.where(lev == -1, _dot_nt(w["qe_cat"], w["kspread"]), 0.0)
            for li, (h, z) in enumerate(zip(levels, w["zs"])):
                q_rows = [(r0, r0 + h) if w["rev"] else (r0 + h, r0 + 2 * h) for r0 in range(0, c, 2 * h)]
                lhs = jnp.concatenate([z[lo:hi] for lo, hi in q_rows], axis=0).astype(BF16)
                sc = _dot_nt(lhs, heads_stacked(z))
                slabs = [a[r:r + h] for r in range(0, c, h)]
                for bi, (lo, hi) in enumerate(q_rows):
                    slabs[lo // h] = jnp.where(lev[lo:hi] == li, sc[bi * h:(bi + 1) * h], slabs[lo // h])
                a = jnp.concatenate(slabs, axis=0)
            w["a"] = a.astype(BF16)

    def outputs():
        for w in work:
            s = w["s"]
            st = st_ref[s]
            stb = st.astype(BF16)
            upd = None
            for hh in range(hpg):
                head = w["grp"] * hpg + hh
                vs = slice(head * DH, (head + 1) * DH)
                v = w["v_ref"][0, w["rows"], vs]
                w["o_ref"][0, w["rows"], vs] = (_dot_nt(hsel(w["qe0"], hh).astype(BF16), stb)
                                                + _dot(w["a"][:, hh * c:(hh + 1) * c], v)).astype(BF16)
                u = _dot_tn(v, hsel(w["kw"], hh).astype(BF16))
                upd = u if upd is None else upd + u
            st_ref[s] = st * w["decay"] + upd

    return [prep, scores, outputs]


def _vec_subchunks(hpg, levels, wall_ref, lev_ref, kmask_ref, data, st_ref):
    nsub = data[0][0].shape[1] // VEC_CHUNK
    return [_vec_stages(hpg, levels, wall_ref, lev_ref, kmask_ref, data, st_ref,
                        (t * VEC_CHUNK, (nsub - 1 - t) * VEC_CHUNK)) for t in range(nsub)]


def _ret_consts(c):
    log_gamma = np.log1p(-(2.0 ** (-5.0 - np.arange(HEADS, dtype=np.float64))))
    t = np.arange(c, dtype=np.float64)
    diff = t[:, None] - t[None, :]
    dmat = np.zeros((2, HEADS, c, c))
    qdec = np.zeros((2, c, MIX_W))
    kdec = np.zeros((2, c, MIX_W))
    for h in range(HEADS):
        lg = log_gamma[h]
        dmat[0, h] = np.where(diff >= 0, np.exp(lg * np.maximum(diff, 0)), 0.0)
        dmat[1, h] = dmat[0, h].T
        sl = slice(h * DH, (h + 1) * DH)
        qdec[0, :, sl] = np.exp(lg * (t + 1))[:, None]
        qdec[1, :, sl] = np.exp(lg * (c - t))[:, None]
        kdec[0, :, sl] = np.exp(lg * (c - 1 - t))[:, None]
        kdec[1, :, sl] = np.exp(lg * t)[:, None]
    sdec = tuple(float(np.exp(lg * c)) for lg in log_gamma)
    return (jnp.asarray(dmat, F32), jnp.asarray(qdec, F32), jnp.asarray(kdec, F32)), sdec


def _ret_stages(sdec, dmat_ref, qdec_ref, kdec_ref, data, st_ref):
    work = []

    def prep():
        for d, (q_ref, k_ref, v_ref, o_ref) in enumerate(data):
            for h in range(HEADS):
                sl = slice(h * DH, (h + 1) * DH)
                s = d * HEADS + h
                q = q_ref[0, :, sl]
                k = k_ref[0, :, sl]
                st = st_ref[s]
                qd = (q.astype(F32) * qdec_ref[d, :, sl]).astype(BF16)
                work.append(dict(d=d, h=h, s=s, sl=sl, o_ref=o_ref, v=v_ref[0, :, sl], st=st,
                                 scores=_dot_nt(q, k), inter=_dot(qd, st.astype(BF16)),
                                 kd=(k.astype(F32) * kdec_ref[d, :, sl]).astype(BF16)))

    def weights():
        for w in work:
            w["att"] = (w["scores"] * dmat_ref[w["d"], w["h"]]).astype(BF16)

    def outputs():
        for w in work:
            w["o_ref"][0, :, w["sl"]] = (w["inter"] + _dot(w["att"], w["v"])).astype(BF16)

    def update():
        for w in work:
            st_ref[w["s"]] = sdec[w["h"]] * w["st"] + _dot_tn(w["kd"], w["v"])

    return [prep, weights, outputs, update]


def _even_scan_kernel(hpg, levels, cum_ref, sel_ref, wall_ref, lev_ref, kmask_ref,
                      aq_f, ak_f, av_f, ag_f, aq_b, ak_b, av_b, ag_b,
                      bq_f, bk_f, bv_f, bg_f, bq_b, bk_b, bv_b, bg_b,
                      ao_f, ao_b, bo_f, bo_b, c_ref, m_ref, st_ref):
    @pl.when(pl.program_id(1) == 0)
    def _():
        c_ref[...] = jnp.zeros_like(c_ref)
        m_ref[...] = jnp.zeros_like(m_ref)
        st_ref[...] = jnp.zeros_like(st_ref)

    mlstm = _mlstm_stages(cum_ref, sel_ref, ((aq_f, ak_f, av_f, ag_f, ao_f), (aq_b, ak_b, av_b, ag_b, ao_b)),
                          c_ref, m_ref)
    gla = _vec_subchunks(hpg, levels, wall_ref, lev_ref, kmask_ref,
                         ((bq_f, bk_f, bv_f, bg_f, bo_f), (bq_b, bk_b, bv_b, bg_b, bo_b)), st_ref)
    _emit_interleaved([mlstm] + gla)


def _even_scan(qa, ka, va, gates, qb, kb, vb, loga_f, loga_b):
    bsz, n, _ = qa.shape
    c = min(SCALAR_CHUNK, n)
    nc, fwd, bwd = _scan_specs(n, c)
    hpg = DH // B_DK
    cum, sel = _mlstm_consts(c)
    wall, levid, kmask, levels = _vec_consts(VEC_CHUNK, hpg)
    consts = (cum, sel, wall, levid, kmask)
    a_w = (MIX_W, MIX_W, MIX_W, LANES)
    b_w = (B_KW, B_KW, MIX_W, B_KW)
    out = jax.ShapeDtypeStruct((bsz, n, MIX_W), BF16)
    return pl.pallas_call(
        functools.partial(_even_scan_kernel, hpg, levels),
        grid=(bsz, nc),
        in_specs=([_const_spec(a.shape) for a in consts]
                  + [fwd(w) for w in a_w] + [bwd(w) for w in a_w]
                  + [fwd(w) for w in b_w] + [bwd(w) for w in b_w]),
        out_specs=[fwd(MIX_W), bwd(MIX_W), fwd(MIX_W), bwd(MIX_W)],
        out_shape=[out] * 4,
        scratch_shapes=[pltpu.VMEM((2 * HEADS, DH, 2 * DH), F32),
                        pltpu.VMEM((2 * HEADS, LANES), F32),
                        pltpu.VMEM((2 * B_KW // LANES, DH, LANES), F32)],
        compiler_params=_params(2),
        name="even_scan",
    )(*consts, qa, ka, va, gates, qa, ka, va, gates, qb, kb, vb, loga_f, qb, kb, vb, loga_b)


def _odd_scan_kernel(levels, sdec, wall_ref, lev_ref, kmask_ref, dmat_ref, qdec_ref, kdec_ref,
                     cq_f, ck_f, cv_f, cg_f, cq_b, ck_b, cv_b, cg_b,
                     rq_f, rk_f, rv_f, rq_b, rk_b, rv_b,
                     co_f, co_b, ro_f, ro_b, cst_ref, rst_ref):
    @pl.when(pl.program_id(1) == 0)
    def _():
        cst_ref[...] = jnp.zeros_like(cst_ref)
        rst_ref[...] = jnp.zeros_like(rst_ref)

    hgrn = _vec_subchunks(1, levels, wall_ref, lev_ref, kmask_ref,
                          ((cq_f, ck_f, cv_f, cg_f, co_f), (cq_b, ck_b, cv_b, cg_b, co_b)), cst_ref)
    ret = _ret_stages(sdec, dmat_ref, qdec_ref, kdec_ref,
                      ((rq_f, rk_f, rv_f, ro_f), (rq_b, rk_b, rv_b, ro_b)), rst_ref)
    _emit_interleaved(hgrn[:1] + [ret] + hgrn[1:])


def _odd_scan(qc, kc_f, kc_b, vc, lf_f, lf_b, qr, kr, vr):
    bsz, n, _ = qc.shape
    c = min(SCALAR_CHUNK, n)
    nc, fwd, bwd = _scan_specs(n, c)
    wall, levid, kmask, levels = _vec_consts(VEC_CHUNK, 1)
    ret_consts, sdec = _ret_consts(c)
    consts = (wall, levid, kmask) + ret_consts
    out = jax.ShapeDtypeStruct((bsz, n, MIX_W), BF16)
    return pl.pallas_call(
        functools.partial(_odd_scan_kernel, levels, sdec),
        grid=(bsz, nc),
        in_specs=([_const_spec(a.shape) for a in consts]
                  + [fwd(MIX_W)] * 4 + [bwd(MIX_W)] * 4 + [fwd(MIX_W)] * 3 + [bwd(MIX_W)] * 3),
        out_specs=[fwd(MIX_W), bwd(MIX_W), fwd(MIX_W), bwd(MIX_W)],
        out_shape=[out] * 4,
        scratch_shapes=[pltpu.VMEM((2 * HEADS, DH, LANES), F32),
                        pltpu.VMEM((2 * HEADS, DH, DH), F32)],
        compiler_params=_params(2),
        name="odd_scan",
    )(*consts, qc, kc_f, vc, lf_f, qc, kc_b, vc, lf_b, qr, kr, vr, qr, kr, vr)


def _mix_ffn_kernel(norms, h1f_ref, h1b_ref, g1_ref, h2f_ref, h2b_ref, g2_ref, x_ref,
                    nw_ref, wmix_ref, lg1_ref, lb1_ref, win_ref, wout_ref, lg2_ref, lb2_ref, o_ref):
    parts = []
    groups = ((h1f_ref, h1b_ref, g1_ref, norms[0], 0), (h2f_ref, h2b_ref, g2_ref, norms[1], MIX_W))
    for hf_ref, hb_ref, gate_ref, norm, off in groups:
        hsum = hf_ref[...].astype(F32) + hb_ref[...].astype(F32)
        gate = gate_ref[...].astype(F32)
        for h in range(HEADS):
            sl = slice(h * DH, (h + 1) * DH)
            y = norm(hsum[:, sl]) * nw_ref[:, off + h * DH:off + (h + 1) * DH] * gate[:, sl]
            parts.append(y.astype(BF16))
    mix = _dot(jnp.concatenate(parts, axis=-1), wmix_ref[...])
    x1 = _layer_norm(ALPHA * x_ref[...] + mix, lg1_ref[...], lb1_ref[...])
    o_ref[...] = _ffn_block(x1, win_ref, wout_ref, lg2_ref[...], lb2_ref[...])


def _mix_ffn(norms, h1f, h1b, g1, h2f, h2b, g2, x2d, nw, w_mix, lg1, lb1, w_in, w_out, lg2, lb2):
    t = x2d.shape[0]
    tm = min(MIX_FFN_TM, t)

    def tile(w):
        return pl.BlockSpec((tm, w), lambda i: (i, 0))

    flat = [a.reshape(t, MIX_W) for a in (h1f, h1b, g1, h2f, h2b, g2)]
    consts = (nw, w_mix, lg1, lb1, w_in, w_out, lg2, lb2)
    return pl.pallas_call(
        functools.partial(_mix_ffn_kernel, norms),
        grid=(t // tm,),
        in_specs=[tile(MIX_W)] * 6 + [tile(D_MODEL)] + [_const_spec(a.shape) for a in consts],
        out_specs=tile(D_MODEL),
        out_shape=jax.ShapeDtypeStruct((t, D_MODEL), F32),
        compiler_params=_params(1),
        name="mix_ffn",
    )(*flat, x2d, *consts)


def _pad_cols(w, width):
    return jnp.pad(w, ((0, 0), (0, width - w.shape[1])))


def _even_weights(w_in, gate_b, conv_w, a2_w, a2_b):
    sizes = (A_W, A_W, A_W, A_W, N_GATES, B_KW, B_KW, MIX_W, MIX_W, 2 * B_RANK)
    offs = np.concatenate([[0], np.cumsum(sizes)])
    aq, ak, av, ao, ag, bq, bk, bv, bg, ba = [w_in[:, offs[i]:offs[i + 1]] for i in range(len(sizes))]
    wqk = jnp.concatenate([aq, ak], axis=1).astype(BF16)
    ag2 = _pad_cols(jnp.concatenate([ag, ag], axis=1), LANES)
    wrest = jnp.concatenate([av, ao, bq, bk, bv, bg, ag2, _pad_cols(ba, LANES)], axis=1).astype(BF16)
    convw = jnp.pad(conv_w, ((0, SUBLANES - CONV_W), (0, 0)))
    gateb = _pad_cols(jnp.concatenate([gate_b, gate_b])[None, :], LANES)
    a2w = jnp.zeros((LANES, 2 * B_KW), F32)
    a2w = a2w.at[0:B_RANK, 0:B_KW].set(a2_w[0]).at[B_RANK:2 * B_RANK, B_KW:].set(a2_w[1]).astype(BF16)
    a2b = jnp.concatenate([a2_b[0], a2_b[1]])[None, :]
    return wqk, wrest, convw, gateb, a2w, a2b


def _rotary_tables(n):
    inv = 1.0 / (ROPE_BASE ** (jnp.arange(0, DH, 2, dtype=F32) / DH))
    ang = jnp.arange(n, dtype=F32)[:, None] * inv[None, :]
    cos, sin = jnp.cos(ang), jnp.sin(ang)
    return jnp.concatenate([cos, cos], -1), jnp.concatenate([-sin, sin], -1)


def _trunk(x, p):
    bsz, n, _ = x.shape
    t = bsz * n
    x2 = x.reshape(t, D_MODEL)
    cos2, sin2 = _rotary_tables(n)
    for l in range(DEPTH):
        j = l // 2
        x2 = _ffn_ln(x2, p["ffn1_in"][l], p["ffn1_out"][l], p["ln_g"][l, 0][None], p["ln_b"][l, 0][None])
        x3 = x2.reshape(bsz, n, D_MODEL)
        if l % 2 == 0:
            qa, ka, va, oga, gates, qb, kb, vb, gb, logaf, logab = _even_in(x3, *p["even"][j])
            h1f, h1b, h2f, h2b = _even_scan(qa, ka, va, gates, qb, kb, vb, logaf, logab)
            norms, g1, g2 = (_head_ln, _head_rms), oga, gb
            nw, w_out = p["ev_nw"][j], p["ev_out"][j]
        else:
            lb = _hgrn_lower_bound(p["od_lb_logits"], j)
            qc, lff, lfb, kcf, kcb, vc, gc, qr, kr, vr, gr = _odd_in(x3, p["od_in"][j], lb, cos2, sin2)
            h1f, h1b, h2f, h2b = _odd_scan(qc, kcf, kcb, vc, lff, lfb, qr, kr, vr)
            norms, g1, g2 = (_head_rms, _head_ln), gc, gr
            nw, w_out = p["od_nw"][j], p["od_out"][j]
        x2 = _mix_ffn(norms, h1f, h1b, g1, h2f, h2b, g2, x2, nw, w_out,
                      p["ln_g"][l, 1][None], p["ln_b"][l, 1][None],
                      p["ffn2_in"][l], p["ffn2_out"][l], p["ln_g"][l, 2][None], p["ln_b"][l, 2][None])
    return x2.reshape(bsz, n, D_MODEL)


def kernel(x_prompt, x_sample, ffn1_w_in, ffn1_w_out, ffn2_w_in, ffn2_w_out, ln_g, ln_b, ev_w_in, ev_gate_b, ev_conv_w, ev_gla_a2_w, ev_gla_a2_b, ev_norm_w, ev_w_out, od_w_in, od_lb_logits, od_norm_w, od_w_out):
    n_even, n_odd = ev_w_in.shape[0], od_w_in.shape[0]
    p = {
        "ffn1_in": ffn1_w_in.astype(BF16), "ffn1_out": ffn1_w_out.astype(BF16),
        "ffn2_in": ffn2_w_in.astype(BF16), "ffn2_out": ffn2_w_out.astype(BF16),
        "ln_g": ln_g, "ln_b": ln_b,
        "even": [_even_weights(ev_w_in[j], ev_gate_b[j], ev_conv_w[j], ev_gla_a2_w[j], ev_gla_a2_b[j])
                 for j in range(n_even)],
        "ev_nw": [ev_norm_w[j][None, :] for j in range(n_even)],
        "ev_out": ev_w_out.astype(BF16),
        "od_in": od_w_in.astype(BF16),
        "od_lb_logits": od_lb_logits,
        "od_nw": [jnp.concatenate([od_norm_w[j], jnp.ones((MIX_W,), F32)])[None, :] for j in range(n_odd)],
        "od_out": od_w_out.astype(BF16),
    }
    return (_trunk(x_prompt, p), _trunk(x_sample, p))
```

```python
import functools
import math

import numpy as np
import jax
import jax.numpy as jnp
from jax import lax
from jax.experimental import pallas as pl
from jax.experimental.pallas import tpu as pltpu

F32 = jnp.float32
BF16 = jnp.bfloat16

D_MODEL = 1024
DEPTH = 4
D_FF = 2816
LN_EPS = 1e-5
ALPHA = (2 * DEPTH) ** 0.25
NEG_BIG = -1e30
TINY = 1e-30
LOG2E = math.log2(math.e)

HEADS = 4
DH = 128
MIX_W = HEADS * DH
A_W = MIX_W
N_GATES = 4 * HEADS
CONV_W = 5
B_DK = 64
B_KW = HEADS * B_DK
B_RANK = 16
GLA_GATE_NORMALIZER = 16.0
ROPE_BASE = 10000.0

LANES = 128
SUBLANES = 8
VMEM_LIMIT = 56 * 1024 * 1024

FFN_TM = 512
MIX_FFN_TM = 512
MIX_FFN_SLABS = 4
FFN_FC = 256
PROJ_TM = 1024
SCALAR_CHUNK = 256
VEC_CHUNK = 128
VEC_SUB = SUBLANES


def _dot(a, b):
    return jnp.dot(a, b, preferred_element_type=F32)


def _dot_nt(a, b):
    return lax.dot_general(a, b, (((1,), (1,)), ((), ())), preferred_element_type=F32)


def _dot_tn(a, b):
    return lax.dot_general(a, b, (((0,), (0,)), ((), ())), preferred_element_type=F32)


def _dot_exact(w_bf16, x, terms=3):
    hi = x.astype(BF16)
    r1 = x - hi.astype(F32)
    mid = r1.astype(BF16)
    out = _dot(w_bf16, hi) + _dot(w_bf16, mid)
    if terms == 3:
        out = out + _dot(w_bf16, (r1 - mid.astype(F32)).astype(BF16))
    return out


def _sigmoid(x):
    return jax.nn.sigmoid(x)


def _silu(x):
    return x * _sigmoid(x)


def _log_sigmoid(x):
    return jnp.minimum(x, 0.0) - jnp.log1p(jnp.exp(-jnp.abs(x)))


def _layer_norm(y, g, b):
    mu = jnp.mean(y, -1, keepdims=True)
    d = y - mu
    var = jnp.mean(d * d, -1, keepdims=True)
    return d * lax.rsqrt(var + LN_EPS) * g + b


def _head_ln(x):
    mu = jnp.mean(x, -1, keepdims=True)
    d = x - mu
    var = jnp.mean(d * d, -1, keepdims=True)
    return d * lax.rsqrt(var + LN_EPS)


def _head_rms(x):
    return x * lax.rsqrt(jnp.mean(x * x, -1, keepdims=True) + LN_EPS)


def _params(n_grid):
    return pltpu.CompilerParams(dimension_semantics=("arbitrary",) * n_grid,
                                vmem_limit_bytes=VMEM_LIMIT)


def _full_spec(shape):
    nd = len(shape)
    return pl.BlockSpec(shape, lambda *_: (0,) * nd)


def _const_spec(shape):
    nd = len(shape)
    return pl.BlockSpec(shape, lambda *_: (0,) * nd, pipeline_mode=pl.Buffered(1))


def _layer_spec(shape, layer):
    nd = len(shape)
    return pl.BlockSpec((None,) + tuple(shape[1:]), lambda *_: (layer,) + (0,) * (nd - 1),
                        pipeline_mode=pl.Buffered(1))


def _lane_tile(x, width):
    return jnp.concatenate([x] * (width // x.shape[-1]), axis=-1)


def _ffn_block(x, win_ref, wout_ref, g, b, between=()):
    xb = x.astype(BF16)
    hs = []
    for j in range(D_FF // FFN_FC):
        lo = j * FFN_FC
        gate = _dot(xb, win_ref[:, lo:lo + FFN_FC])
        up = _dot(xb, win_ref[:, D_FF + lo:D_FF + lo + FFN_FC])
        hs.append((_silu(gate) * up).astype(BF16))
        if j < len(between):
            between[j]()
    acc = _dot(jnp.concatenate(hs, axis=-1), wout_ref[...])
    return _layer_norm(ALPHA * x + 0.5 * acc, g, b)


def _ffn_kernel(x_ref, win_ref, wout_ref, g_ref, b_ref, o_ref):
    o_ref[...] = _ffn_block(x_ref[...], win_ref, wout_ref, g_ref[...], b_ref[...])


def _ffn_ln(x2d, w_in, w_out, layer, g, b):
    t = x2d.shape[0]
    tm = min(FFN_TM, t)
    return pl.pallas_call(
        _ffn_kernel,
        grid=(t // tm,),
        in_specs=[pl.BlockSpec((tm, D_MODEL), lambda i: (i, 0)),
                  _layer_spec(w_in.shape, layer), _layer_spec(w_out.shape, layer),
                  _const_spec(g.shape), _const_spec(b.shape)],
        out_specs=pl.BlockSpec((tm, D_MODEL), lambda i: (i, 0)),
        out_shape=jax.ShapeDtypeStruct((t, D_MODEL), F32),
        compiler_params=_params(1),
        name="ffn_ln",
    )(x2d, w_in, w_out, g, b)


_EV_OFF = {"av": 0, "ao": 512, "bq": 1024, "bk": 1280, "bv": 1536, "bg": 2048, "ag": 2560, "ba": 2688}
HALO = SUBLANES


def _even_in_kernel(xp_ref, xc_ref, xn_ref, wqk_ref, wrest_ref, convw_ref, gateb_ref, a2w_ref, a2b_ref,
                    qa_ref, ka_ref, va_ref, oga_ref, gates_ref, qb_ref, kb_ref, vb_ref, gb_ref,
                    logaf_ref, logab_ref, ybuf):
    i = pl.program_id(1)
    last = pl.num_programs(1) - 1
    tm = xc_ref.shape[1]
    xc = xc_ref[0].astype(BF16)
    xp = jnp.where(i > 0, xp_ref[0], 0.0).astype(BF16)
    xn = jnp.where(i < last, xn_ref[0], 0.0).astype(BF16)
    wqk = wqk_ref[...]
    ybuf[0:HALO, :] = _dot(xp, wqk)
    ybuf[HALO:HALO + tm, :] = _dot(xc, wqk)
    ybuf[HALO + tm:2 * HALO + tm, :] = _dot(xn, wqk)
    base = HALO - CONV_W // 2
    acc = ybuf[pl.ds(base, tm), :] * convw_ref[0:1, :]
    for j in range(1, CONV_W):
        acc = acc + ybuf[pl.ds(base + j, tm), :] * convw_ref[j:j + 1, :]
    qk = _silu(acc)
    qa_ref[0] = qk[:, :A_W].astype(BF16)
    ka_ref[0] = (qk[:, A_W:] * DH ** -0.5).astype(BF16)

    def sect(name, width):
        lo = _EV_OFF[name]
        return _dot(xc, wrest_ref[:, lo:lo + width])

    va_ref[0] = sect("av", MIX_W).astype(BF16)
    oga_ref[0] = _sigmoid(sect("ao", MIX_W)).astype(BF16)
    qb_ref[0] = (sect("bq", B_KW) * B_DK ** -0.5).astype(BF16)
    kb_ref[0] = sect("bk", B_KW).astype(BF16)
    vb_ref[0] = sect("bv", MIX_W).astype(BF16)
    gb_ref[0] = _silu(sect("bg", MIX_W)).astype(BF16)
    gates = sect("ag", LANES) + gateb_ref[...]
    col = lax.broadcasted_iota(jnp.int32, gates.shape, 1)
    is_f = (col % 8) >= 4
    gates_ref[0] = jnp.where(col < 2 * N_GATES, jnp.where(is_f, _log_sigmoid(gates), gates), 0.0)
    ba = sect("ba", LANES).astype(BF16)
    a_pre = _dot(ba, a2w_ref[...]) + a2b_ref[...]
    log_a = _log_sigmoid(a_pre) * (1.0 / GLA_GATE_NORMALIZER)
    logaf_ref[0] = log_a[:, :B_KW]
    logab_ref[0] = log_a[:, B_KW:]


def _even_in(x, wqk, wrest, convw, gateb, a2w, a2b):
    bsz, n, _ = x.shape
    tm = min(PROJ_TM, n)
    nt = n // tm
    r = tm // HALO
    nblk8 = n // HALO

    def tile(w, dt):
        return (pl.BlockSpec((1, tm, w), lambda b, i: (b, i, 0)), jax.ShapeDtypeStruct((bsz, n, w), dt))

    outs = [tile(A_W, BF16), tile(A_W, BF16), tile(MIX_W, BF16), tile(MIX_W, BF16), tile(LANES, F32),
            tile(B_KW, BF16), tile(B_KW, BF16), tile(MIX_W, BF16), tile(MIX_W, BF16),
            tile(B_KW, F32), tile(B_KW, F32)]
    return pl.pallas_call(
        _even_in_kernel,
        grid=(bsz, nt),
        in_specs=[pl.BlockSpec((1, HALO, D_MODEL), lambda b, i: (b, jnp.maximum(i * r - 1, 0), 0)),
                  pl.BlockSpec((1, tm, D_MODEL), lambda b, i: (b, i, 0)),
                  pl.BlockSpec((1, HALO, D_MODEL), lambda b, i: (b, jnp.minimum((i + 1) * r, nblk8 - 1), 0)),
                  _full_spec(wqk.shape), _full_spec(wrest.shape), _full_spec(convw.shape),
                  _full_spec(gateb.shape), _full_spec(a2w.shape), _full_spec(a2b.shape)],
        out_specs=[o[0] for o in outs],
        out_shape=[o[1] for o in outs],
        scratch_shapes=[pltpu.VMEM((tm + 2 * HALO, 2 * A_W), F32)],
        compiler_params=_params(2),
        name="even_in",
    )(x, x, x, wqk, wrest, convw, gateb, a2w, a2b)


def _odd_in_kernel(x_ref, w_ref, lb_ref, cos_ref, sin_ref,
                   qc_ref, lff_ref, lfb_ref, kcf_ref, kcb_ref, vc_ref, gc_ref,
                   qr_ref, kr_ref, vr_ref, gr_ref):
    xb = x_ref[0].astype(BF16)

    def sect(idx):
        return _dot(xb, w_ref[:, idx * MIX_W:(idx + 1) * MIX_W])

    def forget(d, lf_ref, k_ref):
        def emit(z):
            lb = lb_ref[d:d + 1, :]
            f = lb + (1.0 - lb) * _sigmoid(z)
            lf_ref[0] = jnp.log(jnp.maximum(f, TINY))
            k_ref[0] = ((1.0 - lb) * _sigmoid(-z)).astype(BF16)
        return emit

    def rotary(o_ref, scale):
        def emit(y):
            cos2 = cos_ref[...]
            sin2 = sin_ref[...]
            parts = []
            for h in range(HEADS):
                yh = y[:, h * DH:(h + 1) * DH]
                parts.append((yh * cos2 + pltpu.roll(yh, DH // 2, axis=1) * sin2) * scale)
            o_ref[0] = jnp.concatenate(parts, axis=-1).astype(BF16)
        return emit

    def store(o_ref, fn):
        def emit(y):
            o_ref[0] = fn(y).astype(BF16)
        return emit

    epilogues = [store(qc_ref, lambda v: _silu(v) * DH ** -0.5), forget(0, lff_ref, kcf_ref),
                 forget(1, lfb_ref, kcb_ref), store(vc_ref, lambda v: v), store(gc_ref, _silu),
                 rotary(qr_ref, 1.0), rotary(kr_ref, DH ** -0.5), store(vr_ref, lambda v: v),
                 store(gr_ref, _silu)]
    pending = sect(0)
    for idx, epilogue in enumerate(epilogues):
        nxt = sect(idx + 1) if idx + 1 < len(epilogues) else None
        epilogue(pending)
        pending = nxt


def _odd_in(x, w, lb, cos2, sin2):
    bsz, n, _ = x.shape
    tm = min(PROJ_TM, n)

    def tile(dt):
        return (pl.BlockSpec((1, tm, MIX_W), lambda b, i: (b, i, 0)), jax.ShapeDtypeStruct((bsz, n, MIX_W), dt))

    dts = [BF16, F32, F32, BF16, BF16, BF16, BF16, BF16, BF16, BF16, BF16]
    outs = [tile(dt) for dt in dts]
    return pl.pallas_call(
        _odd_in_kernel,
        grid=(bsz, n // tm),
        in_specs=[pl.BlockSpec((1, tm, D_MODEL), lambda b, i: (b, i, 0)),
                  _const_spec(w.shape), _const_spec(lb.shape),
                  pl.BlockSpec((tm, DH), lambda b, i: (i, 0)),
                  pl.BlockSpec((tm, DH), lambda b, i: (i, 0))],
        out_specs=[o[0] for o in outs],
        out_shape=[o[1] for o in outs],
        compiler_params=_params(2),
        name="odd_in",
    )(x, w, lb, cos2, sin2)


def _hgrn_lower_bound(lb_logits, layer_idx):
    n_odd = lb_logits.shape[0]

    def kern(l_ref, o_ref):
        rows = [l_ref[i] for i in range(n_odd)]
        mx = rows[0]
        for r in rows[1:]:
            mx = jnp.maximum(mx, r)
        es = [jnp.exp(r - mx) for r in rows]
        tot = es[0]
        for e in es[1:]:
            tot = tot + e
        acc = jnp.zeros_like(tot)
        for i in range(1, layer_idx + 1):
            acc = acc + es[i] / tot
        o_ref[...] = acc

    return pl.pallas_call(
        kern,
        out_shape=jax.ShapeDtypeStruct(lb_logits.shape[1:], F32),
        name="hgrn_lb",
    )(lb_logits)


def _emit_interleaved(stage_lists):
    for i in range(max(len(st) for st in stage_lists)):
        for st in stage_lists:
            if i < len(st):
                st[i]()


def _scan_specs(n, c):
    nc = n // c

    def fwd(w):
        return pl.BlockSpec((1, c, w), lambda b, i: (b, i, 0))

    def bwd(w):
        return pl.BlockSpec((1, c, w), lambda b, i: (b, nc - 1 - i, 0))

    return nc, fwd, bwd


def _mlstm_stages(cum_ref, sel_ref, data, c_ref, m_ref):
    c = data[0][0].shape[1]
    work = []

    def prep():
        row = lax.broadcasted_iota(jnp.int32, (c, c), 0)
        colm = lax.broadcasted_iota(jnp.int32, (c, c), 1)
        ones_blk = jnp.ones((c, DH), BF16)
        gate_lane = lax.broadcasted_iota(jnp.int32, (1, LANES), 1)
        for d, (q_ref, k_ref, v_ref, g_ref, o_ref) in enumerate(data):
            mask, end = (colm <= row, c - 1) if d == 0 else (colm >= row, 0)
            gates = g_ref[0] * LOG2E
            cum = _dot_exact(cum_ref[d], gates)
            comb = jnp.where(gate_lane < N_GATES, gates, cum)
            comb_hi = comb.astype(BF16)
            xs = jnp.concatenate([comb_hi, (comb - comb_hi.astype(F32)).astype(BF16)], axis=-1)
            gates_t = gates.T
            cum_t = cum.T
            for h in range(HEADS):
                ci, cf = 8 * d + h, 8 * d + 4 + h
                sl = slice(h * DH, (h + 1) * DH)
                s = d * HEADS + h
                q = q_ref[0, :, sl]
                k = k_ref[0, :, sl]
                v_ext = jnp.concatenate([v_ref[0, :, sl], ones_blk], axis=-1)
                rep = _dot(xs, sel_ref[s])
                c_old = c_ref[s]
                work.append(dict(
                    s=s, sl=sl, o_ref=o_ref, mask=mask, k=k, v_ext=v_ext, c_old=c_old,
                    i_rep=rep[:, :DH], b_rep=rep[:, DH:],
                    r_row=gates_t[ci:ci + 1, :] - cum_t[cf:cf + 1, :],
                    tot=rep[end:end + 1, DH:],
                    m_old=m_ref[s:s + 1, :],
                    scores=_dot_nt(q, k), inter=_dot(q, c_old.astype(BF16))))

    def stats():
        for w in work:
            w["dmat"] = jnp.where(w["mask"], _lane_tile(w["b_rep"], c) + w["r_row"], NEG_BIG)
            w["m_inter"] = w["b_rep"] + w["m_old"]
            w["m_t"] = jnp.maximum(w["m_inter"], jnp.max(w["dmat"], axis=-1, keepdims=True))

    def weights():
        for w in work:
            w["att"] = (w["scores"] * jnp.exp2(w["dmat"] - _lane_tile(w["m_t"], c))).astype(BF16)
            w["sc"] = jnp.exp2(w["m_inter"] - w["m_t"])

    def outputs():
        for w in work:
            ext = _lane_tile(w["sc"], 2 * DH) * w["inter"] + _dot(w["att"], w["v_ext"])
            num, den = ext[:, :DH], ext[:, DH:]
            w["o_ref"][0, :, w["sl"]] = (num / jnp.maximum(jnp.abs(den), jnp.exp2(-w["m_t"]))).astype(BF16)

    def update():
        for w in work:
            s = w["s"]
            g_end = w["tot"] + w["i_rep"] - w["b_rep"]
            m_new = jnp.maximum(w["tot"] + w["m_old"], jnp.max(g_end, axis=0, keepdims=True))
            decay = jnp.exp2(w["tot"] + w["m_old"] - m_new)
            wk = (w["k"].astype(F32) * jnp.exp2(g_end - m_new)).astype(BF16)
            c_ref[s] = _lane_tile(decay, 2 * DH) * w["c_old"] + _dot_tn(wk, w["v_ext"])
            m_ref[s:s + 1, :] = m_new

    return [prep, stats, weights, outputs, update]


def _mlstm_consts(c):
    low = np.tril(np.ones((c, c), np.float32))
    cum = np.stack([low, low.T])
    sel = np.zeros((2 * HEADS, 2 * LANES, 2 * DH), np.float32)
    for d in range(2):
        for h in range(HEADS):
            for part in range(2):
                sel[d * HEADS + h, part * LANES + 8 * d + h, :DH] = 1.0
                sel[d * HEADS + h, part * LANES + N_GATES + 8 * d + 4 + h, DH:] = 1.0
    return jnp.asarray(cum, BF16), jnp.asarray(sel, BF16)


def _vec_levels(c):
    lv, h = [], VEC_SUB
    while h < c:
        lv.append(h)
        h *= 2
    return lv


def _vec_consts(c, hpg):
    levels = _vec_levels(c)
    t = np.arange(c)
    low = np.tril(np.ones((c, c), np.float32))
    lev = np.full((c, c), -2, np.int32)
    same_sub = (t[:, None] // VEC_SUB) == (t[None, :] // VEC_SUB)
    lev[same_sub & (t[None, :] <= t[:, None])] = -1
    for li, h in enumerate(levels):
        blk = t // (2 * h)
        second = (t // h) % 2 == 1
        pair = (blk[:, None] == blk[None, :]) & second[:, None] & (~second)[None, :]
        lev[pair] = li
    wall = jnp.asarray(np.stack([low, low.T]), BF16)
    levid = np.stack([np.tile(lev, (1, hpg)), np.tile(lev[::-1, ::-1], (1, hpg))])
    kmask = np.zeros((hpg * c, VEC_SUB * LANES), np.float32)
    lane_head = np.arange(LANES) // (LANES // hpg)
    for hh in range(hpg):
        for j in range(VEC_SUB):
            rows = hh * c + t[t % VEC_SUB == j]
            kmask[np.ix_(rows, j * LANES + np.nonzero(lane_head == hh)[0])] = 1.0
    return wall, jnp.asarray(levid, jnp.int32), jnp.asarray(kmask, BF16), tuple(levels)


def _level_exponent(b, h, reverse):
    pieces = []
    for r0 in range(0, b.shape[0], 2 * h):
        ref = r0 + h if reverse else r0 + h - 1
        bm = b[ref:ref + 1, :]
        first, second = b[r0:r0 + h], b[r0 + h:r0 + 2 * h]
        pieces += [first - bm, bm - second] if reverse else [bm - first, second - bm]
    return jnp.concatenate(pieces, axis=0)


def _vec_stages(hpg, levels, wall_ref, lev_ref, kmask_ref, data, st_ref, row0):
    c = VEC_CHUNK
    ngrp = data[0][0].shape[2] // LANES
    sub = VEC_SUB
    work = []

    def hsel(x, hh):
        if hpg == 1:
            return x
        lane = lax.broadcasted_iota(jnp.int32, (1, LANES), 1)
        return jnp.where(lane // (LANES // hpg) == hh, x, 0.0)

    def heads_stacked(x):
        return jnp.concatenate([hsel(x, hh) for hh in range(hpg)], axis=0).astype(BF16)

    def prep():
        kmask = kmask_ref[...]
        for d, (q_ref, k_ref, v_ref, g_ref, o_ref) in enumerate(data):
            rev = d == 1
            rows = pl.ds(row0[d], c)
            end = 0 if rev else c - 1
            b_all = _dot_exact(wall_ref[d], g_ref[0, rows, :] * LOG2E, terms=2)
            for grp in range(ngrp):
                ls = slice(grp * LANES, (grp + 1) * LANES)
                q = q_ref[0, rows, ls].astype(F32)
                kb = k_ref[0, rows, ls]
                k = kb.astype(F32)
                b = b_all[:, ls]
                b3 = b.reshape(c // sub, sub, LANES)
                parts = []
                for j in range(sub):
                    bj = jnp.broadcast_to(b3[:, j:j + 1, :], b3.shape).reshape(c, LANES)
                    parts.append((q * jnp.exp2(jnp.minimum(b - bj, 0.0))).astype(BF16))
                zs = []
                for h in levels:
                    qk = jnp.concatenate(
                        [x[r0 + i * h:r0 + (i + 1) * h] for r0 in range(0, c, 2 * h)
                         for i, x in enumerate((q, k) if rev else (k, q))], axis=0)
                    zs.append(qk * jnp.exp2(_level_exponent(b, h, rev)))
                b_end = b[end:end + 1, :]
                work.append(dict(
                    d=d, rev=rev, grp=grp, s=d * ngrp + grp, rows=rows, v_ref=v_ref, o_ref=o_ref, zs=zs,
                    qe_cat=jnp.concatenate(parts, axis=-1),
                    kspread=jnp.tile(kb, (hpg, sub)) * kmask,
                    qe0=q * jnp.exp2(b), kw=k * jnp.exp2(b_end - b), decay=jnp.exp2(b_end)))

    def scores():
        for w in work:
            lev = lev_ref[w["d"]]
            a = jnp.where(lev == -1, _dot_nt(w["qe_cat"], w["kspread"]), 0.0)
            for li, (h, z) in enumerate(zip(levels, w["zs"])):
                q_rows = [(r0, r0 + h) if w["rev"] else (r0 + h, r0 + 2 * h) for r0 in range(0, c, 2 * h)]
                lhs = jnp.concatenate([z[lo:hi] for lo, hi in q_rows], axis=0).astype(BF16)
                sc = _dot_nt(lhs, heads_stacked(z))
                slabs = [a[r:r + h] for r in range(0, c, h)]
                for bi, (lo, hi) in enumerate(q_rows):
                    slabs[lo // h] = jnp.where(lev[lo:hi] == li, sc[bi * h:(bi + 1) * h], slabs[lo // h])
                a = jnp.concatenate(slabs, axis=0)
            w["a"] = a.astype(BF16)

    def outputs():
        for w in work:
            s = w["s"]
            st = st_ref[s]
            stb = st.astype(BF16)
            upd = None
            for hh in range(hpg):
                head = w["grp"] * hpg + hh
                vs = slice(head * DH, (head + 1) * DH)
                v = w["v_ref"][0, w["rows"], vs]
                w["o_ref"][0, w["rows"], vs] = (_dot_nt(hsel(w["qe0"], hh).astype(BF16), stb)
                                                + _dot(w["a"][:, hh * c:(hh + 1) * c], v)).astype(BF16)
                u = _dot_tn(v, hsel(w["kw"], hh).astype(BF16))
                upd = u if upd is None else upd + u
            st_ref[s] = st * w["decay"] + upd

    return [prep, scores, outputs]


def _vec_subchunks(hpg, levels, wall_ref, lev_ref, kmask_ref, data, st_ref):
    nsub = data[0][0].shape[1] // VEC_CHUNK
    return [_vec_stages(hpg, levels, wall_ref, lev_ref, kmask_ref, data, st_ref,
                        (t * VEC_CHUNK, (nsub - 1 - t) * VEC_CHUNK)) for t in range(nsub)]


def _ret_consts(c):
    log_gamma = np.log1p(-(2.0 ** (-5.0 - np.arange(HEADS, dtype=np.float64))))
    t = np.arange(c, dtype=np.float64)
    diff = t[:, None] - t[None, :]
    dmat = np.zeros((2, HEADS, c, c))
    qdec = np.zeros((2, c, MIX_W))
    kdec = np.zeros((2, c, MIX_W))
    for h in range(HEADS):
        lg = log_gamma[h]
        dmat[0, h] = np.where(diff >= 0, np.exp(lg * np.maximum(diff, 0)), 0.0)
        dmat[1, h] = dmat[0, h].T
        sl = slice(h * DH, (h + 1) * DH)
        qdec[0, :, sl] = np.exp(lg * (t + 1))[:, None]
        qdec[1, :, sl] = np.exp(lg * (c - t))[:, None]
        kdec[0, :, sl] = np.exp(lg * (c - 1 - t))[:, None]
        kdec[1, :, sl] = np.exp(lg * t)[:, None]
    sdec = tuple(float(np.exp(lg * c)) for lg in log_gamma)
    return (jnp.asarray(dmat, F32), jnp.asarray(qdec, F32), jnp.asarray(kdec, F32)), sdec


def _ret_stages(sdec, dmat_ref, qdec_ref, kdec_ref, data, st_ref):
    work = []

    def prep():
        for d, (q_ref, k_ref, v_ref, o_ref) in enumerate(data):
            for h in range(HEADS):
                sl = slice(h * DH, (h + 1) * DH)
                s = d * HEADS + h
                q = q_ref[0, :, sl]
                k = k_ref[0, :, sl]
                st = st_ref[s]
                qd = (q.astype(F32) * qdec_ref[d, :, sl]).astype(BF16)
                work.append(dict(d=d, h=h, s=s, sl=sl, o_ref=o_ref, v=v_ref[0, :, sl], st=st,
                                 scores=_dot_nt(q, k), inter=_dot(qd, st.astype(BF16)),
                                 kd=(k.astype(F32) * kdec_ref[d, :, sl]).astype(BF16)))

    def weights():
        for w in work:
            w["att"] = (w["scores"] * dmat_ref[w["d"], w["h"]]).astype(BF16)

    def outputs():
        for w in work:
            w["o_ref"][0, :, w["sl"]] = (w["inter"] + _dot(w["att"], w["v"])).astype(BF16)

    def update():
        for w in work:
            st_ref[w["s"]] = sdec[w["h"]] * w["st"] + _dot_tn(w["kd"], w["v"])

    return [prep, weights, outputs, update]


def _even_scan_kernel(hpg, levels, cum_ref, sel_ref, wall_ref, lev_ref, kmask_ref,
                      aq_f, ak_f, av_f, ag_f, aq_b, ak_b, av_b, ag_b,
                      bq_f, bk_f, bv_f, bg_f, bq_b, bk_b, bv_b, bg_b,
                      ao_f, ao_b, bo_f, bo_b, c_ref, m_ref, st_ref):
    @pl.when(pl.program_id(1) == 0)
    def _():
        c_ref[...] = jnp.zeros_like(c_ref)
        m_ref[...] = jnp.zeros_like(m_ref)
        st_ref[...] = jnp.zeros_like(st_ref)

    mlstm = _mlstm_stages(cum_ref, sel_ref, ((aq_f, ak_f, av_f, ag_f, ao_f), (aq_b, ak_b, av_b, ag_b, ao_b)),
                          c_ref, m_ref)
    gla = _vec_subchunks(hpg, levels, wall_ref, lev_ref, kmask_ref,
                         ((bq_f, bk_f, bv_f, bg_f, bo_f), (bq_b, bk_b, bv_b, bg_b, bo_b)), st_ref)
    _emit_interleaved([mlstm] + gla)


def _even_scan(qa, ka, va, gates, qb, kb, vb, loga_f, loga_b):
    bsz, n, _ = qa.shape
    c = min(SCALAR_CHUNK, n)
    nc, fwd, bwd = _scan_specs(n, c)
    hpg = DH // B_DK
    cum, sel = _mlstm_consts(c)
    wall, levid, kmask, levels = _vec_consts(VEC_CHUNK, hpg)
    consts = (cum, sel, wall, levid, kmask)
    a_w = (MIX_W, MIX_W, MIX_W, LANES)
    b_w = (B_KW, B_KW, MIX_W, B_KW)
    out = jax.ShapeDtypeStruct((bsz, n, MIX_W), BF16)
    return pl.pallas_call(
        functools.partial(_even_scan_kernel, hpg, levels),
        grid=(bsz, nc),
        in_specs=([_const_spec(a.shape) for a in consts]
                  + [fwd(w) for w in a_w] + [bwd(w) for w in a_w]
                  + [fwd(w) for w in b_w] + [bwd(w) for w in b_w]),
        out_specs=[fwd(MIX_W), bwd(MIX_W), fwd(MIX_W), bwd(MIX_W)],
        out_shape=[out] * 4,
        scratch_shapes=[pltpu.VMEM((2 * HEADS, DH, 2 * DH), F32),
                        pltpu.VMEM((2 * HEADS, LANES), F32),
                        pltpu.VMEM((2 * B_KW // LANES, DH, LANES), F32)],
        compiler_params=_params(2),
        name="even_scan",
    )(*consts, qa, ka, va, gates, qa, ka, va, gates, qb, kb, vb, loga_f, qb, kb, vb, loga_b)


def _odd_scan_kernel(levels, sdec, wall_ref, lev_ref, kmask_ref, dmat_ref, qdec_ref, kdec_ref,
                     cq_f, ck_f, cv_f, cg_f, cq_b, ck_b, cv_b, cg_b,
                     rq_f, rk_f, rv_f, rq_b, rk_b, rv_b,
                     co_f, co_b, ro_f, ro_b, cst_ref, rst_ref):
    @pl.when(pl.program_id(1) == 0)
    def _():
        cst_ref[...] = jnp.zeros_like(cst_ref)
        rst_ref[...] = jnp.zeros_like(rst_ref)

    hgrn = _vec_subchunks(1, levels, wall_ref, lev_ref, kmask_ref,
                          ((cq_f, ck_f, cv_f, cg_f, co_f), (cq_b, ck_b, cv_b, cg_b, co_b)), cst_ref)
    ret = _ret_stages(sdec, dmat_ref, qdec_ref, kdec_ref,
                      ((rq_f, rk_f, rv_f, ro_f), (rq_b, rk_b, rv_b, ro_b)), rst_ref)
    _emit_interleaved(hgrn[:1] + [ret] + hgrn[1:])


def _odd_scan(qc, kc_f, kc_b, vc, lf_f, lf_b, qr, kr, vr):
    bsz, n, _ = qc.shape
    c = min(SCALAR_CHUNK, n)
    nc, fwd, bwd = _scan_specs(n, c)
    wall, levid, kmask, levels = _vec_consts(VEC_CHUNK, 1)
    ret_consts, sdec = _ret_consts(c)
    consts = (wall, levid, kmask) + ret_consts
    out = jax.ShapeDtypeStruct((bsz, n, MIX_W), BF16)
    return pl.pallas_call(
        functools.partial(_odd_scan_kernel, levels, sdec),
        grid=(bsz, nc),
        in_specs=([_const_spec(a.shape) for a in consts]
                  + [fwd(MIX_W)] * 4 + [bwd(MIX_W)] * 4 + [fwd(MIX_W)] * 3 + [bwd(MIX_W)] * 3),
        out_specs=[fwd(MIX_W), bwd(MIX_W), fwd(MIX_W), bwd(MIX_W)],
        out_shape=[out] * 4,
        scratch_shapes=[pltpu.VMEM((2 * HEADS, DH, LANES), F32),
                        pltpu.VMEM((2 * HEADS, DH, DH), F32)],
        compiler_params=_params(2),
        name="odd_scan",
    )(*consts, qc, kc_f, vc, lf_f, qc, kc_b, vc, lf_b, qr, kr, vr, qr, kr, vr)


def _mix_ffn_kernel(norms, h1f_ref, h1b_ref, g1_ref, h2f_ref, h2b_ref, g2_ref, x_ref,
                    nw_ref, wmix_ref, lg1_ref, lb1_ref, win_ref, wout_ref, lg2_ref, lb2_ref, o_ref, x1_ref):
    s = pl.program_id(0)

    @pl.when(s == 0)
    def _():
        x1_ref[...] = jnp.zeros_like(x1_ref)

    cur = s % 2
    slab = x_ref.shape[0] // MIX_FFN_SLABS
    groups = ((h1f_ref, h1b_ref, g1_ref, norms[0], 0), (h2f_ref, h2b_ref, g2_ref, norms[1], MIX_W))

    vals = {}

    def gate_stage(i):
        def emit():
            rows = pl.ds(i * slab, slab)
            parts = []
            for hf_ref, hb_ref, gate_ref, norm, off in groups:
                hsum = hf_ref[rows, :].astype(F32) + hb_ref[rows, :].astype(F32)
                gate = gate_ref[rows, :].astype(F32)
                for h in range(HEADS):
                    sl = slice(h * DH, (h + 1) * DH)
                    y = norm(hsum[:, sl]) * nw_ref[:, off + h * DH:off + (h + 1) * DH] * gate[:, sl]
                    parts.append(y.astype(BF16))
            vals[i] = jnp.concatenate(parts, axis=-1)
        return emit

    def proj_stage(i):
        def emit():
            vals[i] = _dot(vals[i], wmix_ref[...])
        return emit

    def norm_stage(i):
        def emit():
            rows = pl.ds(i * slab, slab)
            x1_ref[cur, rows, :] = _layer_norm(ALPHA * x_ref[rows, :] + vals[i], lg1_ref[...], lb1_ref[...])
        return emit

    slots = [[] for _ in range(2 * MIX_FFN_SLABS + 1)]
    for i in range(MIX_FFN_SLABS):
        slots[2 * i].append(gate_stage(i))
        slots[2 * i + 1].append(proj_stage(i))
        slots[2 * i + 2].append(norm_stage(i))
    between = [lambda fs=fs: [f() for f in fs] for fs in slots]
    o_ref[...] = _ffn_block(x1_ref[1 - cur], win_ref, wout_ref, lg2_ref[...], lb2_ref[...], between)


def _mix_ffn(norms, h1f, h1b, g1, h2f, h2b, g2, x2d, nw, w_mix, lg1, lb1, w_in, w_out, layer, lg2, lb2):
    t = x2d.shape[0]
    tm = min(MIX_FFN_TM, t)
    nt = t // tm

    def tile(w):
        return pl.BlockSpec((tm, w), lambda s: (jnp.minimum(s, nt - 1), 0))

    flat = [a.reshape(t, MIX_W) for a in (h1f, h1b, g1, h2f, h2b, g2)]
    consts = (nw, w_mix, lg1, lb1, w_in, w_out, lg2, lb2)
    return pl.pallas_call(
        functools.partial(_mix_ffn_kernel, norms),
        grid=(nt + 1,),
        in_specs=([tile(MIX_W)] * 6 + [tile(D_MODEL)]
                  + [_layer_spec(a.shape, layer) if a.ndim == 3 else _const_spec(a.shape) for a in consts]),
        out_specs=pl.BlockSpec((tm, D_MODEL), lambda s: (jnp.maximum(s - 1, 0), 0)),
        out_shape=jax.ShapeDtypeStruct((t, D_MODEL), F32),
        scratch_shapes=[pltpu.VMEM((2, tm, D_MODEL), F32)],
        compiler_params=_params(1),
        name="mix_ffn",
    )(*flat, x2d, *consts)


def _pad_cols(w, width):
    return jnp.pad(w, ((0, 0), (0, width - w.shape[1])))


def _even_weights(w_in, gate_b, conv_w, a2_w, a2_b):
    sizes = (A_W, A_W, A_W, A_W, N_GATES, B_KW, B_KW, MIX_W, MIX_W, 2 * B_RANK)
    offs = np.concatenate([[0], np.cumsum(sizes)])
    aq, ak, av, ao, ag, bq, bk, bv, bg, ba = [w_in[:, offs[i]:offs[i + 1]] for i in range(len(sizes))]
    wqk = jnp.concatenate([aq, ak], axis=1).astype(BF16)
    ag2 = _pad_cols(jnp.concatenate([ag, ag], axis=1), LANES)
    wrest = jnp.concatenate([av, ao, bq, bk, bv, bg, ag2, _pad_cols(ba, LANES)], axis=1).astype(BF16)
    convw = jnp.pad(conv_w, ((0, SUBLANES - CONV_W), (0, 0)))
    gateb = _pad_cols(jnp.concatenate([gate_b, gate_b])[None, :], LANES)
    a2w = jnp.zeros((LANES, 2 * B_KW), F32)
    a2w = a2w.at[0:B_RANK, 0:B_KW].set(a2_w[0]).at[B_RANK:2 * B_RANK, B_KW:].set(a2_w[1]).astype(BF16)
    a2b = jnp.concatenate([a2_b[0], a2_b[1]])[None, :]
    return wqk, wrest, convw, gateb, a2w, a2b


def _rotary_tables(n):
    inv = 1.0 / (ROPE_BASE ** (jnp.arange(0, DH, 2, dtype=F32) / DH))
    ang = jnp.arange(n, dtype=F32)[:, None] * inv[None, :]
    cos, sin = jnp.cos(ang), jnp.sin(ang)
    return jnp.concatenate([cos, cos], -1), jnp.concatenate([-sin, sin], -1)


def _trunk(x, p):
    bsz, n, _ = x.shape
    t = bsz * n
    x2 = x.reshape(t, D_MODEL)
    cos2, sin2 = _rotary_tables(n)
    for l in range(DEPTH):
        j = l // 2
        x2 = _ffn_ln(x2, p["ffn1_in"], p["ffn1_out"], l, p["ln_g"][l, 0][None], p["ln_b"][l, 0][None])
        x3 = x2.reshape(bsz, n, D_MODEL)
        if l % 2 == 0:
            qa, ka, va, oga, gates, qb, kb, vb, gb, logaf, logab = _even_in(x3, *p["even"][j])
            h1f, h1b, h2f, h2b = _even_scan(qa, ka, va, gates, qb, kb, vb, logaf, logab)
            norms, g1, g2 = (_head_ln, _head_rms), oga, gb
            nw, w_out = p["ev_nw"][j], p["ev_out"][j]
        else:
            lb = _hgrn_lower_bound(p["od_lb_logits"], j)
            qc, lff, lfb, kcf, kcb, vc, gc, qr, kr, vr, gr = _odd_in(x3, p["od_in"][j], lb, cos2, sin2)
            h1f, h1b, h2f, h2b = _odd_scan(qc, kcf, kcb, vc, lff, lfb, qr, kr, vr)
            norms, g1, g2 = (_head_rms, _head_ln), gc, gr
            nw, w_out = p["od_nw"][j], p["od_out"][j]
        x2 = _mix_ffn(norms, h1f, h1b, g1, h2f, h2b, g2, x2, nw, w_out,
                      p["ln_g"][l, 1][None], p["ln_b"][l, 1][None],
                      p["ffn2_in"], p["ffn2_out"], l, p["ln_g"][l, 2][None], p["ln_b"][l, 2][None])
    return x2.reshape(bsz, n, D_MODEL)


def kernel(x_prompt, x_sample, ffn1_w_in, ffn1_w_out, ffn2_w_in, ffn2_w_out, ln_g, ln_b, ev_w_in, ev_gate_b, ev_conv_w, ev_gla_a2_w, ev_gla_a2_b, ev_norm_w, ev_w_out, od_w_in, od_lb_logits, od_norm_w, od_w_out):
    n_even, n_odd = ev_w_in.shape[0], od_w_in.shape[0]
    p = {
        "ffn1_in": ffn1_w_in.astype(BF16), "ffn1_out": ffn1_w_out.astype(BF16),
        "ffn2_in": ffn2_w_in.astype(BF16), "ffn2_out": ffn2_w_out.astype(BF16),
        "ln_g": ln_g, "ln_b": ln_b,
        "even": [_even_weights(ev_w_in[j], ev_gate_b[j], ev_conv_w[j], ev_gla_a2_w[j], ev_gla_a2_b[j])
                 for j in range(n_even)],
        "ev_nw": [ev_norm_w[j][None, :] for j in range(n_even)],
        "ev_out": ev_w_out.astype(BF16),
        "od_in": od_w_in.astype(BF16),
        "od_lb_logits": od_lb_logits,
        "od_nw": [jnp.concatenate([od_norm_w[j], jnp.ones((MIX_W,), F32)])[None, :] for j in range(n_odd)],
        "od_out": od_w_out.astype(BF16),
    }
    return (_trunk(x_prompt, p), _trunk(x_sample, p))
```

```python
import functools
import math

import numpy as np
import jax
import jax.numpy as jnp
from jax import lax
from jax.experimental import pallas as pl
from jax.experimental.pallas import tpu as pltpu

F32 = jnp.float32
BF16 = jnp.bfloat16

D_MODEL = 1024
DEPTH = 4
D_FF = 2816
LN_EPS = 1e-5
ALPHA = (2 * DEPTH) ** 0.25
NEG_BIG = -1e30
TINY = 1e-30
LOG2E = math.log2(math.e)

HEADS = 4
DH = 128
MIX_W = HEADS * DH
A_W = MIX_W
N_GATES = 4 * HEADS
CONV_W = 5
B_DK = 64
B_KW = HEADS * B_DK
B_RANK = 16
GLA_GATE_NORMALIZER = 16.0
ROPE_BASE = 10000.0

LANES = 128
SUBLANES = 8
VMEM_LIMIT = 56 * 1024 * 1024

FFN_TM = 1024
FFN_PIECE = 256
MIX_FFN_TM = 512
MIX_FFN_SLABS = 4
FFN_FC = 256
PROJ_TM = 1024
PROJ_PIECE = 256
SCALAR_CHUNK = 256
VEC_CHUNK = 128
VEC_SUB = SUBLANES


def _dot(a, b):
    return jnp.dot(a, b, preferred_element_type=F32)


def _dot_nt(a, b):
    return lax.dot_general(a, b, (((1,), (1,)), ((), ())), preferred_element_type=F32)


def _dot_tn(a, b):
    return lax.dot_general(a, b, (((0,), (0,)), ((), ())), preferred_element_type=F32)


def _dot_exact(w_bf16, x, terms=3):
    hi = x.astype(BF16)
    r1 = x - hi.astype(F32)
    mid = r1.astype(BF16)
    out = _dot(w_bf16, hi) + _dot(w_bf16, mid)
    if terms == 3:
        out = out + _dot(w_bf16, (r1 - mid.astype(F32)).astype(BF16))
    return out


def _sigmoid(x):
    return jax.nn.sigmoid(x)


def _silu(x):
    return x * _sigmoid(x)


def _log_sigmoid(x):
    return jnp.minimum(x, 0.0) - jnp.log1p(jnp.exp(-jnp.abs(x)))


def _layer_norm(y, g, b):
    mu = jnp.mean(y, -1, keepdims=True)
    d = y - mu
    var = jnp.mean(d * d, -1, keepdims=True)
    return d * lax.rsqrt(var + LN_EPS) * g + b


def _head_ln(x):
    mu = jnp.mean(x, -1, keepdims=True)
    d = x - mu
    var = jnp.mean(d * d, -1, keepdims=True)
    return d * lax.rsqrt(var + LN_EPS)


def _head_rms(x):
    return x * lax.rsqrt(jnp.mean(x * x, -1, keepdims=True) + LN_EPS)


def _params(n_grid):
    return pltpu.CompilerParams(dimension_semantics=("arbitrary",) * n_grid,
                                vmem_limit_bytes=VMEM_LIMIT)


def _full_spec(shape):
    nd = len(shape)
    return pl.BlockSpec(shape, lambda *_: (0,) * nd)


def _const_spec(shape):
    nd = len(shape)
    return pl.BlockSpec(shape, lambda *_: (0,) * nd, pipeline_mode=pl.Buffered(1))


def _layer_spec(shape, layer):
    nd = len(shape)
    return pl.BlockSpec((None,) + tuple(shape[1:]), lambda *_: (layer,) + (0,) * (nd - 1),
                        pipeline_mode=pl.Buffered(1))


def _lane_tile(x, width):
    return jnp.concatenate([x] * (width // x.shape[-1]), axis=-1)


def _ffn_block(x, win_ref, wout_ref, g, b, between=()):
    xb = x.astype(BF16)
    hs = []
    for j in range(D_FF // FFN_FC):
        lo = j * FFN_FC
        gate = _dot(xb, win_ref[:, lo:lo + FFN_FC])
        up = _dot(xb, win_ref[:, D_FF + lo:D_FF + lo + FFN_FC])
        hs.append((_silu(gate) * up).astype(BF16))
        if j < len(between):
            between[j]()
    acc = _dot(jnp.concatenate(hs, axis=-1), wout_ref[...])
    return _layer_norm(ALPHA * x + 0.5 * acc, g, b)


def _ffn_kernel(x_ref, win_ref, wout_ref, g_ref, b_ref, o_ref):
    for r in range(x_ref.shape[0] // FFN_PIECE):
        rows = pl.ds(r * FFN_PIECE, FFN_PIECE)
        o_ref[rows, :] = _ffn_block(x_ref[rows, :], win_ref, wout_ref, g_ref[...], b_ref[...])


def _ffn_ln(x2d, w_in, w_out, layer, g, b):
    t = x2d.shape[0]
    tm = min(FFN_TM, t)
    return pl.pallas_call(
        _ffn_kernel,
        grid=(t // tm,),
        in_specs=[pl.BlockSpec((tm, D_MODEL), lambda i: (i, 0)),
                  _layer_spec(w_in.shape, layer), _layer_spec(w_out.shape, layer),
                  _const_spec(g.shape), _const_spec(b.shape)],
        out_specs=pl.BlockSpec((tm, D_MODEL), lambda i: (i, 0)),
        out_shape=jax.ShapeDtypeStruct((t, D_MODEL), F32),
        compiler_params=_params(1),
        name="ffn_ln",
    )(x2d, w_in, w_out, g, b)


_EV_OFF = {"av": 0, "ao": 512, "bq": 1024, "bk": 1280, "bv": 1536, "bg": 2048, "ag": 2560, "ba": 2688}
HALO = SUBLANES


def _even_in_kernel(xp_ref, xc_ref, xn_ref, wqk_ref, wrest_ref, convw_ref, gateb_ref, a2w_ref, a2b_ref,
                    qa_ref, ka_ref, va_ref, oga_ref, gates_ref, qb_ref, kb_ref, vb_ref, gb_ref,
                    logaf_ref, logab_ref, ybuf):
    i = pl.program_id(1)
    last = pl.num_programs(1) - 1
    tm = xc_ref.shape[1]
    piece = min(PROJ_PIECE, tm)
    xp = jnp.where(i > 0, xp_ref[0], 0.0).astype(BF16)
    xn = jnp.where(i < last, xn_ref[0], 0.0).astype(BF16)
    wqk = wqk_ref[...]
    ybuf[0:HALO, :] = _dot(xp, wqk)
    ybuf[HALO + tm:2 * HALO + tm, :] = _dot(xn, wqk)
    for r0 in range(0, tm, piece):
        ybuf[HALO + r0:HALO + r0 + piece, :] = _dot(xc_ref[0, r0:r0 + piece, :].astype(BF16), wqk)
    for r0 in range(0, tm, piece):
        rows = pl.ds(r0, piece)
        win = ybuf[r0:r0 + piece + 2 * HALO, :]
        acc = None
        for j in range(CONV_W):
            shift = (CONV_W // 2 - j) % (piece + 2 * HALO)
            yj = win if shift == 0 else pltpu.roll(win, shift, axis=0)
            term = yj[HALO:HALO + piece] * convw_ref[j:j + 1, :]
            acc = term if acc is None else acc + term
        qk = _silu(acc)
        qa_ref[0, rows, :] = qk[:, :A_W].astype(BF16)
        ka_ref[0, rows, :] = (qk[:, A_W:] * DH ** -0.5).astype(BF16)
        xc = xc_ref[0, rows, :].astype(BF16)

        def sect(name, width):
            lo = _EV_OFF[name]
            return _dot(xc, wrest_ref[:, lo:lo + width])

        va_ref[0, rows, :] = sect("av", MIX_W).astype(BF16)
        oga_ref[0, rows, :] = _sigmoid(sect("ao", MIX_W)).astype(BF16)
        qb_ref[0, rows, :] = (sect("bq", B_KW) * B_DK ** -0.5).astype(BF16)
        kb_ref[0, rows, :] = sect("bk", B_KW).astype(BF16)
        vb_ref[0, rows, :] = sect("bv", MIX_W).astype(BF16)
        gb_ref[0, rows, :] = _silu(sect("bg", MIX_W)).astype(BF16)
        gates = sect("ag", LANES) + gateb_ref[...]
        col = lax.broadcasted_iota(jnp.int32, gates.shape, 1)
        is_f = (col % 8) >= 4
        gates_ref[0, rows, :] = jnp.where(col < 2 * N_GATES, jnp.where(is_f, _log_sigmoid(gates), gates), 0.0)
        ba = sect("ba", LANES).astype(BF16)
        a_pre = _dot(ba, a2w_ref[...]) + a2b_ref[...]
        log_a = _log_sigmoid(a_pre) * (1.0 / GLA_GATE_NORMALIZER)
        logaf_ref[0, rows, :] = log_a[:, :B_KW]
        logab_ref[0, rows, :] = log_a[:, B_KW:]


def _even_in(x, wqk, wrest, convw, gateb, a2w, a2b):
    bsz, n, _ = x.shape
    tm = min(PROJ_TM, n)
    nt = n // tm
    r = tm // HALO
    nblk8 = n // HALO

    def tile(w, dt):
        return (pl.BlockSpec((1, tm, w), lambda b, i: (b, i, 0)), jax.ShapeDtypeStruct((bsz, n, w), dt))

    outs = [tile(A_W, BF16), tile(A_W, BF16), tile(MIX_W, BF16), tile(MIX_W, BF16), tile(LANES, F32),
            tile(B_KW, BF16), tile(B_KW, BF16), tile(MIX_W, BF16), tile(MIX_W, BF16),
            tile(B_KW, F32), tile(B_KW, F32)]
    return pl.pallas_call(
        _even_in_kernel,
        grid=(bsz, nt),
        in_specs=[pl.BlockSpec((1, HALO, D_MODEL), lambda b, i: (b, jnp.maximum(i * r - 1, 0), 0)),
                  pl.BlockSpec((1, tm, D_MODEL), lambda b, i: (b, i, 0)),
                  pl.BlockSpec((1, HALO, D_MODEL), lambda b, i: (b, jnp.minimum((i + 1) * r, nblk8 - 1), 0)),
                  _full_spec(wqk.shape), _full_spec(wrest.shape), _full_spec(convw.shape),
                  _full_spec(gateb.shape), _full_spec(a2w.shape), _full_spec(a2b.shape)],
        out_specs=[o[0] for o in outs],
        out_shape=[o[1] for o in outs],
        scratch_shapes=[pltpu.VMEM((tm + 2 * HALO, 2 * A_W), F32)],
        compiler_params=_params(2),
        name="even_in",
    )(x, x, x, wqk, wrest, convw, gateb, a2w, a2b)


def _odd_in_kernel(x_ref, w_ref, lb_ref, cos_ref, sin_ref,
                   qc_ref, lff_ref, lfb_ref, kcf_ref, kcb_ref, vc_ref, gc_ref,
                   qr_ref, kr_ref, vr_ref, gr_ref):
    for r0 in range(0, x_ref.shape[1], PROJ_PIECE):
        rows = pl.ds(r0, PROJ_PIECE)
        xb = x_ref[0, rows, :].astype(BF16)

        def sect(idx):
            return _dot(xb, w_ref[:, idx * MIX_W:(idx + 1) * MIX_W])

        qc_ref[0, rows, :] = (_silu(sect(0)) * DH ** -0.5).astype(BF16)
        for d, (lf_ref, k_ref) in enumerate(((lff_ref, kcf_ref), (lfb_ref, kcb_ref))):
            z = sect(1 + d)
            lb = lb_ref[d:d + 1, :]
            f = lb + (1.0 - lb) * _sigmoid(z)
            lf_ref[0, rows, :] = jnp.log(jnp.maximum(f, TINY))
            k_ref[0, rows, :] = ((1.0 - lb) * _sigmoid(-z)).astype(BF16)
        vc_ref[0, rows, :] = sect(3).astype(BF16)
        gc_ref[0, rows, :] = _silu(sect(4)).astype(BF16)
        cos2 = cos_ref[rows, :]
        sin2 = sin_ref[rows, :]

        def rotary(y, scale):
            parts = []
            for h in range(HEADS):
                yh = y[:, h * DH:(h + 1) * DH]
                parts.append((yh * cos2 + pltpu.roll(yh, DH // 2, axis=1) * sin2) * scale)
            return jnp.concatenate(parts, axis=-1).astype(BF16)

        qr_ref[0, rows, :] = rotary(sect(5), 1.0)
        kr_ref[0, rows, :] = rotary(sect(6), DH ** -0.5)
        vr_ref[0, rows, :] = sect(7).astype(BF16)
        gr_ref[0, rows, :] = _silu(sect(8)).astype(BF16)


def _odd_in(x, w, lb, cos2, sin2):
    bsz, n, _ = x.shape
    tm = min(PROJ_TM, n)

    def tile(dt):
        return (pl.BlockSpec((1, tm, MIX_W), lambda b, i: (b, i, 0)), jax.ShapeDtypeStruct((bsz, n, MIX_W), dt))

    dts = [BF16, F32, F32, BF16, BF16, BF16, BF16, BF16, BF16, BF16, BF16]
    outs = [tile(dt) for dt in dts]
    return pl.pallas_call(
        _odd_in_kernel,
        grid=(bsz, n // tm),
        in_specs=[pl.BlockSpec((1, tm, D_MODEL), lambda b, i: (b, i, 0)),
                  _const_spec(w.shape), _const_spec(lb.shape),
                  pl.BlockSpec((tm, DH), lambda b, i: (i, 0)),
                  pl.BlockSpec((tm, DH), lambda b, i: (i, 0))],
        out_specs=[o[0] for o in outs],
        out_shape=[o[1] for o in outs],
        compiler_params=_params(2),
        name="odd_in",
    )(x, w, lb, cos2, sin2)


def _hgrn_lower_bound(lb_logits, layer_idx):
    n_odd = lb_logits.shape[0]

    def kern(l_ref, o_ref):
        rows = [l_ref[i] for i in range(n_odd)]
        mx = rows[0]
        for r in rows[1:]:
            mx = jnp.maximum(mx, r)
        es = [jnp.exp(r - mx) for r in rows]
        tot = es[0]
        for e in es[1:]:
            tot = tot + e
        acc = jnp.zeros_like(tot)
        for i in range(1, layer_idx + 1):
            acc = acc + es[i] / tot
        o_ref[...] = acc

    return pl.pallas_call(
        kern,
        out_shape=jax.ShapeDtypeStruct(lb_logits.shape[1:], F32),
        name="hgrn_lb",
    )(lb_logits)


def _emit_interleaved(stage_lists):
    for i in range(max(len(st) for st in stage_lists)):
        for st in stage_lists:
            if i < len(st):
                st[i]()


def _scan_specs(n, c):
    nc = n // c

    def fwd(w):
        return pl.BlockSpec((1, c, w), lambda b, i: (b, i, 0))

    def bwd(w):
        return pl.BlockSpec((1, c, w), lambda b, i: (b, nc - 1 - i, 0))

    return nc, fwd, bwd


def _mlstm_stages(cum_ref, sel_ref, data, c_ref, m_ref):
    c = data[0][0].shape[1]
    work = []

    def prep():
        row = lax.broadcasted_iota(jnp.int32, (c, c), 0)
        colm = lax.broadcasted_iota(jnp.int32, (c, c), 1)
        ones_blk = jnp.ones((c, DH), BF16)
        gate_lane = lax.broadcasted_iota(jnp.int32, (1, LANES), 1)
        for d, (q_ref, k_ref, v_ref, g_ref, o_ref) in enumerate(data):
            mask, end = (colm <= row, c - 1) if d == 0 else (colm >= row, 0)
            gates = g_ref[0] * LOG2E
            cum = _dot_exact(cum_ref[d], gates)
            comb = jnp.where(gate_lane < N_GATES, gates, cum)
            comb_hi = comb.astype(BF16)
            xs = jnp.concatenate([comb_hi, (comb - comb_hi.astype(F32)).astype(BF16)], axis=-1)
            gates_t = gates.T
            cum_t = cum.T
            for h in range(HEADS):
                ci, cf = 8 * d + h, 8 * d + 4 + h
                sl = slice(h * DH, (h + 1) * DH)
                s = d * HEADS + h
                q = q_ref[0, :, sl]
                k = k_ref[0, :, sl]
                v_ext = jnp.concatenate([v_ref[0, :, sl], ones_blk], axis=-1)
                rep = _dot(xs, sel_ref[s])
                c_old = c_ref[s]
                work.append(dict(
                    s=s, sl=sl, o_ref=o_ref, mask=mask, k=k, v_ext=v_ext, c_old=c_old,
                    i_rep=rep[:, :DH], b_rep=rep[:, DH:],
                    r_row=gates_t[ci:ci + 1, :] - cum_t[cf:cf + 1, :],
                    tot=rep[end:end + 1, DH:],
                    m_old=m_ref[s:s + 1, :],
                    scores=_dot_nt(q, k), inter=_dot(q, c_old.astype(BF16))))

    def stats():
        for w in work:
            w["dmat"] = jnp.where(w["mask"], _lane_tile(w["b_rep"], c) + w["r_row"], NEG_BIG)
            w["m_inter"] = w["b_rep"] + w["m_old"]
            w["m_t"] = jnp.maximum(w["m_inter"], jnp.max(w["dmat"], axis=-1, keepdims=True))

    def weights():
        for w in work:
            w["att"] = (w["scores"] * jnp.exp2(w["dmat"] - _lane_tile(w["m_t"], c))).astype(BF16)
            w["sc"] = jnp.exp2(w["m_inter"] - w["m_t"])

    def outputs():
        for w in work:
            ext = _lane_tile(w["sc"], 2 * DH) * w["inter"] + _dot(w["att"], w["v_ext"])
            num, den = ext[:, :DH], ext[:, DH:]
            w["o_ref"][0, :, w["sl"]] = (num / jnp.maximum(jnp.abs(den), jnp.exp2(-w["m_t"]))).astype(BF16)

    def update():
        for w in work:
            s = w["s"]
            g_end = w["tot"] + w["i_rep"] - w["b_rep"]
            m_new = jnp.maximum(w["tot"] + w["m_old"], jnp.max(g_end, axis=0, keepdims=True))
            decay = jnp.exp2(w["tot"] + w["m_old"] - m_new)
            wk = (w["k"].astype(F32) * jnp.exp2(g_end - m_new)).astype(BF16)
            c_ref[s] = _lane_tile(decay, 2 * DH) * w["c_old"] + _dot_tn(wk, w["v_ext"])
            m_ref[s:s + 1, :] = m_new

    return [prep, stats, weights, outputs, update]


def _mlstm_consts(c):
    low = np.tril(np.ones((c, c), np.float32))
    cum = np.stack([low, low.T])
    sel = np.zeros((2 * HEADS, 2 * LANES, 2 * DH), np.float32)
    for d in range(2):
        for h in range(HEADS):
            for part in range(2):
                sel[d * HEADS + h, part * LANES + 8 * d + h, :DH] = 1.0
                sel[d * HEADS + h, part * LANES + N_GATES + 8 * d + 4 + h, DH:] = 1.0
    return jnp.asarray(cum, BF16), jnp.asarray(sel, BF16)


def _vec_levels(c):
    lv, h = [], VEC_SUB
    while h < c:
        lv.append(h)
        h *= 2
    return lv


def _vec_consts(c, hpg):
    levels = _vec_levels(c)
    t = np.arange(c)
    low = np.tril(np.ones((c, c), np.float32))
    lev = np.full((c, c), -2, np.int32)
    same_sub = (t[:, None] // VEC_SUB) == (t[None, :] // VEC_SUB)
    lev[same_sub & (t[None, :] <= t[:, None])] = -1
    for li, h in enumerate(levels):
        blk = t // (2 * h)
        second = (t // h) % 2 == 1
        pair = (blk[:, None] == blk[None, :]) & second[:, None] & (~second)[None, :]
        lev[pair] = li
    wall = jnp.asarray(np.stack([low, low.T]), BF16)
    levid = np.stack([np.tile(lev, (1, hpg)), np.tile(lev[::-1, ::-1], (1, hpg))])
    kmask = np.zeros((hpg * c, VEC_SUB * LANES), np.float32)
    lane_head = np.arange(LANES) // (LANES // hpg)
    for hh in range(hpg):
        for j in range(VEC_SUB):
            rows = hh * c + t[t % VEC_SUB == j]
            kmask[np.ix_(rows, j * LANES + np.nonzero(lane_head == hh)[0])] = 1.0
    return wall, jnp.asarray(levid, jnp.int32), jnp.asarray(kmask, BF16), tuple(levels)


def _level_exponent(b, h, reverse):
    pieces = []
    for r0 in range(0, b.shape[0], 2 * h):
        ref = r0 + h if reverse else r0 + h - 1
        bm = b[ref:ref + 1, :]
        first, second = b[r0:r0 + h], b[r0 + h:r0 + 2 * h]
        pieces += [first - bm, bm - second] if reverse else [bm - first, second - bm]
    return jnp.concatenate(pieces, axis=0)


def _vec_stages(hpg, levels, wall_ref, lev_ref, kmask_ref, data, st_ref, row0):
    c = VEC_CHUNK
    ngrp = data[0][0].shape[2] // LANES
    sub = VEC_SUB
    work = []

    def hsel(x, hh):
        if hpg == 1:
            return x
        lane = lax.broadcasted_iota(jnp.int32, (1, LANES), 1)
        return jnp.where(lane // (LANES // hpg) == hh, x, 0.0)

    def heads_stacked(x):
        return jnp.concatenate([hsel(x, hh) for hh in range(hpg)], axis=0).astype(BF16)

    def prep():
        kmask = kmask_ref[...]
        for d, (q_ref, k_ref, v_ref, g_ref, o_ref) in enumerate(data):
            rev = d == 1
            rows = pl.ds(row0[d], c)
            end = 0 if rev else c - 1
            b_all = _dot_exact(wall_ref[d], g_ref[0, rows, :] * LOG2E, terms=2)
            for grp in range(ngrp):
                ls = slice(grp * LANES, (grp + 1) * LANES)
                q = q_ref[0, rows, ls].astype(F32)
                kb = k_ref[0, rows, ls]
                k = kb.astype(F32)
                b = b_all[:, ls]
                b3 = b.reshape(c // sub, sub, LANES)
                parts = []
                for j in range(sub):
                    bj = jnp.broadcast_to(b3[:, j:j + 1, :], b3.shape).reshape(c, LANES)
                    parts.append((q * jnp.exp2(jnp.minimum(b - bj, 0.0))).astype(BF16))
                zs = []
                for h in levels:
                    qk = jnp.concatenate(
                        [x[r0 + i * h:r0 + (i + 1) * h] for r0 in range(0, c, 2 * h)
                         for i, x in enumerate((q, k) if rev else (k, q))], axis=0)
                    zs.append(qk * jnp.exp2(_level_exponent(b, h, rev)))
                b_end = b[end:end + 1, :]
                work.append(dict(
                    d=d, rev=rev, grp=grp, s=d * ngrp + grp, rows=rows, v_ref=v_ref, o_ref=o_ref, zs=zs,
                    qe_cat=jnp.concatenate(parts, axis=-1),
                    kspread=jnp.tile(kb, (hpg, sub)) * kmask,
                    qe0=q * jnp.exp2(b), kw=k * jnp.exp2(b_end - b), decay=jnp.exp2(b_end)))

    def scores():
        for w in work:
            lev = lev_ref[w["d"]]
            a = jnp.where(lev == -1, _dot_nt(w["qe_cat"], w["kspread"]), 0.0)
            for li, (h, z) in enumerate(zip(levels, w["zs"])):
                q_rows = [(r0, r0 + h) if w["rev"] else (r0 + h, r0 + 2 * h) for r0 in range(0, c, 2 * h)]
                lhs = jnp.concatenate([z[lo:hi] for lo, hi in q_rows], axis=0).astype(BF16)
                sc = _dot_nt(lhs, heads_stacked(z))
                slabs = [a[r:r + h] for r in range(0, c, h)]
                for bi, (lo, hi) in enumerate(q_rows):
                    slabs[lo // h] = jnp.where(lev[lo:hi] == li, sc[bi * h:(bi + 1) * h], slabs[lo // h])
                a = jnp.concatenate(slabs, axis=0)
            w["a"] = a.astype(BF16)

    def outputs():
        for w in work:
            s = w["s"]
            st = st_ref[s]
            stb = st.astype(BF16)
            upd = None
            for hh in range(hpg):
                head = w["grp"] * hpg + hh
                vs = slice(head * DH, (head + 1) * DH)
                v = w["v_ref"][0, w["rows"], vs]
                w["o_ref"][0, w["rows"], vs] = (_dot_nt(hsel(w["qe0"], hh).astype(BF16), stb)
                                                + _dot(w["a"][:, hh * c:(hh + 1) * c], v)).astype(BF16)
                u = _dot_tn(v, hsel(w["kw"], hh).astype(BF16))
                upd = u if upd is None else upd + u
            st_ref[s] = st * w["decay"] + upd

    return [prep, scores, outputs]


def _vec_subchunks(hpg, levels, wall_ref, lev_ref, kmask_ref, data, st_ref):
    nsub = data[0][0].shape[1] // VEC_CHUNK
    return [_vec_stages(hpg, levels, wall_ref, lev_ref, kmask_ref, data, st_ref,
                        (t * VEC_CHUNK, (nsub - 1 - t) * VEC_CHUNK)) for t in range(nsub)]


def _ret_consts(c):
    log_gamma = np.log1p(-(2.0 ** (-5.0 - np.arange(HEADS, dtype=np.float64))))
    t = np.arange(c, dtype=np.float64)
    diff = t[:, None] - t[None, :]
    dmat = np.zeros((2, HEADS, c, c))
    qdec = np.zeros((2, c, MIX_W))
    kdec = np.zeros((2, c, MIX_W))
    for h in range(HEADS):
        lg = log_gamma[h]
        dmat[0, h] = np.where(diff >= 0, np.exp(lg * np.maximum(diff, 0)), 0.0)
        dmat[1, h] = dmat[0, h].T
        sl = slice(h * DH, (h + 1) * DH)
        qdec[0, :, sl] = np.exp(lg * (t + 1))[:, None]
        qdec[1, :, sl] = np.exp(lg * (c - t))[:, None]
        kdec[0, :, sl] = np.exp(lg * (c - 1 - t))[:, None]
        kdec[1, :, sl] = np.exp(lg * t)[:, None]
    sdec = tuple(float(np.exp(lg * c)) for lg in log_gamma)
    return (jnp.asarray(dmat, F32), jnp.asarray(qdec, F32), jnp.asarray(kdec, F32)), sdec


def _ret_stages(sdec, dmat_ref, qdec_ref, kdec_ref, data, st_ref):
    work = []

    def prep():
        for d, (q_ref, k_ref, v_ref, o_ref) in enumerate(data):
            for h in range(HEADS):
                sl = slice(h * DH, (h + 1) * DH)
                s = d * HEADS + h
                q = q_ref[0, :, sl]
                k = k_ref[0, :, sl]
                st = st_ref[s]
                qd = (q.astype(F32) * qdec_ref[d, :, sl]).astype(BF16)
                work.append(dict(d=d, h=h, s=s, sl=sl, o_ref=o_ref, v=v_ref[0, :, sl], st=st,
                                 scores=_dot_nt(q, k), inter=_dot(qd, st.astype(BF16)),
                                 kd=(k.astype(F32) * kdec_ref[d, :, sl]).astype(BF16)))

    def weights():
        for w in work:
            w["att"] = (w["scores"] * dmat_ref[w["d"], w["h"]]).astype(BF16)

    def outputs():
        for w in work:
            w["o_ref"][0, :, w["sl"]] = (w["inter"] + _dot(w["att"], w["v"])).astype(BF16)

    def update():
        for w in work:
            st_ref[w["s"]] = sdec[w["h"]] * w["st"] + _dot_tn(w["kd"], w["v"])

    return [prep, weights, outputs, update]


def _even_scan_kernel(hpg, levels, cum_ref, sel_ref, wall_ref, lev_ref, kmask_ref,
                      aq_f, ak_f, av_f, ag_f, aq_b, ak_b, av_b, ag_b,
                      bq_f, bk_f, bv_f, bg_f, bq_b, bk_b, bv_b, bg_b,
                      ao_f, ao_b, bo_f, bo_b, c_ref, m_ref, st_ref):
    @pl.when(pl.program_id(1) == 0)
    def _():
        c_ref[...] = jnp.zeros_like(c_ref)
        m_ref[...] = jnp.zeros_like(m_ref)
        st_ref[...] = jnp.zeros_like(st_ref)

    mlstm = _mlstm_stages(cum_ref, sel_ref, ((aq_f, ak_f, av_f, ag_f, ao_f), (aq_b, ak_b, av_b, ag_b, ao_b)),
                          c_ref, m_ref)
    gla = _vec_subchunks(hpg, levels, wall_ref, lev_ref, kmask_ref,
                         ((bq_f, bk_f, bv_f, bg_f, bo_f), (bq_b, bk_b, bv_b, bg_b, bo_b)), st_ref)
    _emit_interleaved([mlstm] + gla)


def _even_scan(qa, ka, va, gates, qb, kb, vb, loga_f, loga_b):
    bsz, n, _ = qa.shape
    c = min(SCALAR_CHUNK, n)
    nc, fwd, bwd = _scan_specs(n, c)
    hpg = DH // B_DK
    cum, sel = _mlstm_consts(c)
    wall, levid, kmask, levels = _vec_consts(VEC_CHUNK, hpg)
    consts = (cum, sel, wall, levid, kmask)
    a_w = (MIX_W, MIX_W, MIX_W, LANES)
    b_w = (B_KW, B_KW, MIX_W, B_KW)
    out = jax.ShapeDtypeStruct((bsz, n, MIX_W), BF16)
    return pl.pallas_call(
        functools.partial(_even_scan_kernel, hpg, levels),
        grid=(bsz, nc),
        in_specs=([_const_spec(a.shape) for a in consts]
                  + [fwd(w) for w in a_w] + [bwd(w) for w in a_w]
                  + [fwd(w) for w in b_w] + [bwd(w) for w in b_w]),
        out_specs=[fwd(MIX_W), bwd(MIX_W), fwd(MIX_W), bwd(MIX_W)],
        out_shape=[out] * 4,
        scratch_shapes=[pltpu.VMEM((2 * HEADS, DH, 2 * DH), F32),
                        pltpu.VMEM((2 * HEADS, LANES), F32),
                        pltpu.VMEM((2 * B_KW // LANES, DH, LANES), F32)],
        compiler_params=_params(2),
        name="even_scan",
    )(*consts, qa, ka, va, gates, qa, ka, va, gates, qb, kb, vb, loga_f, qb, kb, vb, loga_b)


def _odd_scan_kernel(levels, sdec, wall_ref, lev_ref, kmask_ref, dmat_ref, qdec_ref, kdec_ref,
                     cq_f, ck_f, cv_f, cg_f, cq_b, ck_b, cv_b, cg_b,
                     rq_f, rk_f, rv_f, rq_b, rk_b, rv_b,
                     co_f, co_b, ro_f, ro_b, cst_ref, rst_ref):
    @pl.when(pl.program_id(1) == 0)
    def _():
        cst_ref[...] = jnp.zeros_like(cst_ref)
        rst_ref[...] = jnp.zeros_like(rst_ref)

    hgrn = _vec_subchunks(1, levels, wall_ref, lev_ref, kmask_ref,
                          ((cq_f, ck_f, cv_f, cg_f, co_f), (cq_b, ck_b, cv_b, cg_b, co_b)), cst_ref)
    ret = _ret_stages(sdec, dmat_ref, qdec_ref, kdec_ref,
                      ((rq_f, rk_f, rv_f, ro_f), (rq_b, rk_b, rv_b, ro_b)), rst_ref)
    _emit_interleaved(hgrn[:1] + [ret] + hgrn[1:])


def _odd_scan(qc, kc_f, kc_b, vc, lf_f, lf_b, qr, kr, vr):
    bsz, n, _ = qc.shape
    c = min(SCALAR_CHUNK, n)
    nc, fwd, bwd = _scan_specs(n, c)
    wall, levid, kmask, levels = _vec_consts(VEC_CHUNK, 1)
    ret_consts, sdec = _ret_consts(c)
    consts = (wall, levid, kmask) + ret_consts
    out = jax.ShapeDtypeStruct((bsz, n, MIX_W), BF16)
    return pl.pallas_call(
        functools.partial(_odd_scan_kernel, levels, sdec),
        grid=(bsz, nc),
        in_specs=([_const_spec(a.shape) for a in consts]
                  + [fwd(MIX_W)] * 4 + [bwd(MIX_W)] * 4 + [fwd(MIX_W)] * 3 + [bwd(MIX_W)] * 3),
        out_specs=[fwd(MIX_W), bwd(MIX_W), fwd(MIX_W), bwd(MIX_W)],
        out_shape=[out] * 4,
        scratch_shapes=[pltpu.VMEM((2 * HEADS, DH, LANES), F32),
                        pltpu.VMEM((2 * HEADS, DH, DH), F32)],
        compiler_params=_params(2),
        name="odd_scan",
    )(*consts, qc, kc_f, vc, lf_f, qc, kc_b, vc, lf_b, qr, kr, vr, qr, kr, vr)


def _mix_ffn_kernel(norms, h1f_ref, h1b_ref, g1_ref, h2f_ref, h2b_ref, g2_ref, x_ref,
                    nw_ref, wmix_ref, lg1_ref, lb1_ref, win_ref, wout_ref, lg2_ref, lb2_ref, o_ref, x1_ref):
    s = pl.program_id(0)

    @pl.when(s == 0)
    def _():
        x1_ref[...] = jnp.zeros_like(x1_ref)

    cur = s % 2
    slab = x_ref.shape[0] // MIX_FFN_SLABS
    groups = ((h1f_ref, h1b_ref, g1_ref, norms[0], 0), (h2f_ref, h2b_ref, g2_ref, norms[1], MIX_W))

    vals = {}

    def gate_stage(i):
        def emit():
            rows = pl.ds(i * slab, slab)
            parts = []
            for hf_ref, hb_ref, gate_ref, norm, off in groups:
                hsum = hf_ref[rows, :].astype(F32) + hb_ref[rows, :].astype(F32)
                gate = gate_ref[rows, :].astype(F32)
                for h in range(HEADS):
                    sl = slice(h * DH, (h + 1) * DH)
                    y = norm(hsum[:, sl]) * nw_ref[:, off + h * DH:off + (h + 1) * DH] * gate[:, sl]
                    parts.append(y.astype(BF16))
            vals[i] = jnp.concatenate(parts, axis=-1)
        return emit

    def proj_stage(i):
        def emit():
            vals[i] = _dot(vals[i], wmix_ref[...])
        return emit

    def norm_stage(i):
        def emit():
            rows = pl.ds(i * slab, slab)
            x1_ref[cur, rows, :] = _layer_norm(ALPHA * x_ref[rows, :] + vals[i], lg1_ref[...], lb1_ref[...])
        return emit

    slots = [[] for _ in range(2 * MIX_FFN_SLABS + 1)]
    for i in range(MIX_FFN_SLABS):
        slots[2 * i].append(gate_stage(i))
        slots[2 * i + 1].append(proj_stage(i))
        slots[2 * i + 2].append(norm_stage(i))
    between = [lambda fs=fs: [f() for f in fs] for fs in slots]
    o_ref[...] = _ffn_block(x1_ref[1 - cur], win_ref, wout_ref, lg2_ref[...], lb2_ref[...], between)


def _mix_ffn(norms, h1f, h1b, g1, h2f, h2b, g2, x2d, nw, w_mix, lg1, lb1, w_in, w_out, layer, lg2, lb2):
    t = x2d.shape[0]
    tm = min(MIX_FFN_TM, t)
    nt = t // tm

    def tile(w):
        return pl.BlockSpec((tm, w), lambda s: (jnp.minimum(s, nt - 1), 0))

    flat = [a.reshape(t, MIX_W) for a in (h1f, h1b, g1, h2f, h2b, g2)]
    consts = (nw, w_mix, lg1, lb1, w_in, w_out, lg2, lb2)
    return pl.pallas_call(
        functools.partial(_mix_ffn_kernel, norms),
        grid=(nt + 1,),
        in_specs=([tile(MIX_W)] * 6 + [tile(D_MODEL)]
                  + [_layer_spec(a.shape, layer) if a.ndim == 3 else _const_spec(a.shape) for a in consts]),
        out_specs=pl.BlockSpec((tm, D_MODEL), lambda s: (jnp.maximum(s - 1, 0), 0)),
        out_shape=jax.ShapeDtypeStruct((t, D_MODEL), F32),
        scratch_shapes=[pltpu.VMEM((2, tm, D_MODEL), F32)],
        compiler_params=_params(1),
        name="mix_ffn",
    )(*flat, x2d, *consts)


def _pad_cols(w, width):
    return jnp.pad(w, ((0, 0), (0, width - w.shape[1])))


def _even_weights(w_in, gate_b, conv_w, a2_w, a2_b):
    sizes = (A_W, A_W, A_W, A_W, N_GATES, B_KW, B_KW, MIX_W, MIX_W, 2 * B_RANK)
    offs = np.concatenate([[0], np.cumsum(sizes)])
    aq, ak, av, ao, ag, bq, bk, bv, bg, ba = [w_in[:, offs[i]:offs[i + 1]] for i in range(len(sizes))]
    wqk = jnp.concatenate([aq, ak], axis=1).astype(BF16)
    ag2 = _pad_cols(jnp.concatenate([ag, ag], axis=1), LANES)
    wrest = jnp.concatenate([av, ao, bq, bk, bv, bg, ag2, _pad_cols(ba, LANES)], axis=1).astype(BF16)
    convw = jnp.pad(conv_w, ((0, SUBLANES - CONV_W), (0, 0)))
    gateb = _pad_cols(jnp.concatenate([gate_b, gate_b])[None, :], LANES)
    a2w = jnp.zeros((LANES, 2 * B_KW), F32)
    a2w = a2w.at[0:B_RANK, 0:B_KW].set(a2_w[0]).at[B_RANK:2 * B_RANK, B_KW:].set(a2_w[1]).astype(BF16)
    a2b = jnp.concatenate([a2_b[0], a2_b[1]])[None, :]
    return wqk, wrest, convw, gateb, a2w, a2b


def _rotary_tables(n):
    inv = 1.0 / (ROPE_BASE ** (jnp.arange(0, DH, 2, dtype=F32) / DH))
    ang = jnp.arange(n, dtype=F32)[:, None] * inv[None, :]
    cos, sin = jnp.cos(ang), jnp.sin(ang)
    return jnp.concatenate([cos, cos], -1), jnp.concatenate([-sin, sin], -1)


def _trunk(x, p):
    bsz, n, _ = x.shape
    t = bsz * n
    x2 = x.reshape(t, D_MODEL)
    cos2, sin2 = _rotary_tables(n)
    for l in range(DEPTH):
        j = l // 2
        x2 = _ffn_ln(x2, p["ffn1_in"], p["ffn1_out"], l, p["ln_g"][l, 0][None], p["ln_b"][l, 0][None])
        x3 = x2.reshape(bsz, n, D_MODEL)
        if l % 2 == 0:
            qa, ka, va, oga, gates, qb, kb, vb, gb, logaf, logab = _even_in(x3, *p["even"][j])
            h1f, h1b, h2f, h2b = _even_scan(qa, ka, va, gates, qb, kb, vb, logaf, logab)
            norms, g1, g2 = (_head_ln, _head_rms), oga, gb
            nw, w_out = p["ev_nw"][j], p["ev_out"][j]
        else:
            lb = _hgrn_lower_bound(p["od_lb_logits"], j)
            qc, lff, lfb, kcf, kcb, vc, gc, qr, kr, vr, gr = _odd_in(x3, p["od_in"][j], lb, cos2, sin2)
            h1f, h1b, h2f, h2b = _odd_scan(qc, kcf, kcb, vc, lff, lfb, qr, kr, vr)
            norms, g1, g2 = (_head_rms, _head_ln), gc, gr
            nw, w_out = p["od_nw"][j], p["od_out"][j]
        x2 = _mix_ffn(norms, h1f, h1b, g1, h2f, h2b, g2, x2, nw, w_out,
                      p["ln_g"][l, 1][None], p["ln_b"][l, 1][None],
                      p["ffn2_in"], p["ffn2_out"], l, p["ln_g"][l, 2][None], p["ln_b"][l, 2][None])
    return x2.reshape(bsz, n, D_MODEL)


def kernel(x_prompt, x_sample, ffn1_w_in, ffn1_w_out, ffn2_w_in, ffn2_w_out, ln_g, ln_b, ev_w_in, ev_gate_b, ev_conv_w, ev_gla_a2_w, ev_gla_a2_b, ev_norm_w, ev_w_out, od_w_in, od_lb_logits, od_norm_w, od_w_out):
    n_even, n_odd = ev_w_in.shape[0], od_w_in.shape[0]
    p = {
        "ffn1_in": ffn1_w_in.astype(BF16), "ffn1_out": ffn1_w_out.astype(BF16),
        "ffn2_in": ffn2_w_in.astype(BF16), "ffn2_out": ffn2_w_out.astype(BF16),
        "ln_g": ln_g, "ln_b": ln_b,
        "even": [_even_weights(ev_w_in[j], ev_gate_b[j], ev_conv_w[j], ev_gla_a2_w[j], ev_gla_a2_b[j])
                 for j in range(n_even)],
        "ev_nw": [ev_norm_w[j][None, :] for j in range(n_even)],
        "ev_out": ev_w_out.astype(BF16),
        "od_in": od_w_in.astype(BF16),
        "od_lb_logits": od_lb_logits,
        "od_nw": [jnp.concatenate([od_norm_w[j], jnp.ones((MIX_W,), F32)])[None, :] for j in range(n_odd)],
        "od_out": od_w_out.astype(BF16),
    }
    return (_trunk(x_prompt, p), _trunk(x_sample, p))
```

```python
import functools
import math

import numpy as np
import jax
import jax.numpy as jnp
from jax import lax
from jax.experimental import pallas as pl
from jax.experimental.pallas import tpu as pltpu

F32 = jnp.float32
BF16 = jnp.bfloat16

D_MODEL = 1024
DEPTH = 4
D_FF = 2816
LN_EPS = 1e-5
ALPHA = (2 * DEPTH) ** 0.25
NEG_BIG = -1e30
TINY = 1e-30
LOG2E = math.log2(math.e)

HEADS = 4
DH = 128
MIX_W = HEADS * DH
A_W = MIX_W
N_GATES = 4 * HEADS
CONV_W = 5
B_DK = 64
B_KW = HEADS * B_DK
B_RANK = 16
GLA_GATE_NORMALIZER = 16.0
ROPE_BASE = 10000.0

LANES = 128
SUBLANES = 8
VMEM_LIMIT = 56 * 1024 * 1024

FFN_TM = 1024
FFN_PIECE = 256
MIX_FFN_TM = 512
MIX_FFN_SLABS = 4
FFN_FC = 256
PROJ_TM = 1024
PROJ_PIECE = 256
SCALAR_CHUNK = 256
VEC_CHUNK = 128
VEC_SUB = SUBLANES


def _dot(a, b):
    return jnp.dot(a, b, preferred_element_type=F32)


def _dot_nt(a, b):
    return lax.dot_general(a, b, (((1,), (1,)), ((), ())), preferred_element_type=F32)


def _dot_tn(a, b):
    return lax.dot_general(a, b, (((0,), (0,)), ((), ())), preferred_element_type=F32)


def _dot_exact(w_bf16, x, terms=3):
    hi = x.astype(BF16)
    r1 = x - hi.astype(F32)
    mid = r1.astype(BF16)
    out = _dot(w_bf16, hi) + _dot(w_bf16, mid)
    if terms == 3:
        out = out + _dot(w_bf16, (r1 - mid.astype(F32)).astype(BF16))
    return out


def _sigmoid(x):
    return jax.nn.sigmoid(x)


def _silu(x):
    return x * _sigmoid(x)


def _log_sigmoid(x):
    return jnp.minimum(x, 0.0) - jnp.log1p(jnp.exp(-jnp.abs(x)))


def _layer_norm(y, g, b):
    mu = jnp.mean(y, -1, keepdims=True)
    d = y - mu
    var = jnp.mean(d * d, -1, keepdims=True)
    return d * lax.rsqrt(var + LN_EPS) * g + b


def _head_ln(x):
    mu = jnp.mean(x, -1, keepdims=True)
    d = x - mu
    var = jnp.mean(d * d, -1, keepdims=True)
    return d * lax.rsqrt(var + LN_EPS)


def _head_rms(x):
    return x * lax.rsqrt(jnp.mean(x * x, -1, keepdims=True) + LN_EPS)


def _params(n_grid):
    return pltpu.CompilerParams(dimension_semantics=("arbitrary",) * n_grid,
                                vmem_limit_bytes=VMEM_LIMIT)


def _full_spec(shape):
    nd = len(shape)
    return pl.BlockSpec(shape, lambda *_: (0,) * nd)


def _const_spec(shape):
    nd = len(shape)
    return pl.BlockSpec(shape, lambda *_: (0,) * nd, pipeline_mode=pl.Buffered(1))


def _layer_spec(shape, layer):
    nd = len(shape)
    return pl.BlockSpec((None,) + tuple(shape[1:]), lambda *_: (layer,) + (0,) * (nd - 1),
                        pipeline_mode=pl.Buffered(1))


class _Cols:
    def __init__(self, ref, off, width):
        self.ref, self.off, self.width = ref, off, width
        self.shape = (ref.shape[0], ref.shape[1], width)

    def _index(self, idx):
        if not isinstance(idx, tuple):
            idx = (idx, slice(None), slice(None))
        lead, rows, lanes = idx
        lo = self.off + (lanes.start or 0)
        hi = self.off + (self.width if lanes.stop is None else lanes.stop)
        return lead, rows, slice(lo, hi)

    def __getitem__(self, idx):
        return self.ref[self._index(idx)]

    def __setitem__(self, idx, value):
        self.ref[self._index(idx)] = value


def _lane_tile(x, width):
    return jnp.concatenate([x] * (width // x.shape[-1]), axis=-1)


def _ffn_block(x, win_ref, wout_ref, g, b, between=()):
    xb = x.astype(BF16)
    hs = []
    for j in range(D_FF // FFN_FC):
        lo = j * FFN_FC
        gate = _dot(xb, win_ref[:, lo:lo + FFN_FC])
        up = _dot(xb, win_ref[:, D_FF + lo:D_FF + lo + FFN_FC])
        hs.append((_silu(gate) * up).astype(BF16))
        if j < len(between):
            between[j]()
    acc = _dot(jnp.concatenate(hs, axis=-1), wout_ref[...])
    return _layer_norm(ALPHA * x + 0.5 * acc, g, b)


def _ffn_kernel(x_ref, win_ref, wout_ref, g_ref, b_ref, o_ref):
    for r in range(x_ref.shape[0] // FFN_PIECE):
        rows = pl.ds(r * FFN_PIECE, FFN_PIECE)
        o_ref[rows, :] = _ffn_block(x_ref[rows, :], win_ref, wout_ref, g_ref[...], b_ref[...])


def _ffn_ln(x2d, w_in, w_out, layer, g, b):
    t = x2d.shape[0]
    tm = min(FFN_TM, t)
    return pl.pallas_call(
        _ffn_kernel,
        grid=(t // tm,),
        in_specs=[pl.BlockSpec((tm, D_MODEL), lambda i: (i, 0)),
                  _layer_spec(w_in.shape, layer), _layer_spec(w_out.shape, layer),
                  _const_spec(g.shape), _const_spec(b.shape)],
        out_specs=pl.BlockSpec((tm, D_MODEL), lambda i: (i, 0)),
        out_shape=jax.ShapeDtypeStruct((t, D_MODEL), F32),
        compiler_params=_params(1),
        name="ffn_ln",
    )(x2d, w_in, w_out, g, b)


_EV_OFF = {"av": 0, "ao": 512, "bq": 1024, "bk": 1280, "bv": 1536, "bg": 2048, "ag": 2560, "ba": 2688}
HALO = SUBLANES


def _even_in_kernel(xp_ref, xc_ref, xn_ref, wqk_ref, wrest_ref, convw_ref, gateb_ref, a2w_ref, a2b_ref,
                    a_ref, oga_ref, gates_ref, b_ref, gb_ref, logaf_ref, logab_ref, ybuf):
    qa_ref, ka_ref, va_ref = (_Cols(a_ref, j * MIX_W, MIX_W) for j in range(3))
    qb_ref, kb_ref, vb_ref = _Cols(b_ref, 0, B_KW), _Cols(b_ref, B_KW, B_KW), _Cols(b_ref, 2 * B_KW, MIX_W)
    i = pl.program_id(1)
    last = pl.num_programs(1) - 1
    tm = xc_ref.shape[1]
    piece = min(PROJ_PIECE, tm)
    xp = jnp.where(i > 0, xp_ref[0], 0.0).astype(BF16)
    xn = jnp.where(i < last, xn_ref[0], 0.0).astype(BF16)
    wqk = wqk_ref[...]
    ybuf[0:HALO, :] = _dot(xp, wqk)
    ybuf[HALO + tm:2 * HALO + tm, :] = _dot(xn, wqk)
    for r0 in range(0, tm, piece):
        ybuf[HALO + r0:HALO + r0 + piece, :] = _dot(xc_ref[0, r0:r0 + piece, :].astype(BF16), wqk)
    for r0 in range(0, tm, piece):
        rows = pl.ds(r0, piece)
        win = ybuf[r0:r0 + piece + 2 * HALO, :]
        acc = None
        for j in range(CONV_W):
            shift = (CONV_W // 2 - j) % (piece + 2 * HALO)
            yj = win if shift == 0 else pltpu.roll(win, shift, axis=0)
            term = yj[HALO:HALO + piece] * convw_ref[j:j + 1, :]
            acc = term if acc is None else acc + term
        qk = _silu(acc)
        qa_ref[0, rows, :] = qk[:, :A_W].astype(BF16)
        ka_ref[0, rows, :] = (qk[:, A_W:] * DH ** -0.5).astype(BF16)
        xc = xc_ref[0, rows, :].astype(BF16)

        def sect(name, width):
            lo = _EV_OFF[name]
            return _dot(xc, wrest_ref[:, lo:lo + width])

        va_ref[0, rows, :] = sect("av", MIX_W).astype(BF16)
        oga_ref[0, rows, :] = _sigmoid(sect("ao", MIX_W)).astype(BF16)
        qb_ref[0, rows, :] = (sect("bq", B_KW) * B_DK ** -0.5).astype(BF16)
        kb_ref[0, rows, :] = sect("bk", B_KW).astype(BF16)
        vb_ref[0, rows, :] = sect("bv", MIX_W).astype(BF16)
        gb_ref[0, rows, :] = _silu(sect("bg", MIX_W)).astype(BF16)
        gates = sect("ag", LANES) + gateb_ref[...]
        col = lax.broadcasted_iota(jnp.int32, gates.shape, 1)
        is_f = (col % 8) >= 4
        gates_ref[0, rows, :] = jnp.where(col < 2 * N_GATES, jnp.where(is_f, _log_sigmoid(gates), gates), 0.0)
        ba = sect("ba", LANES).astype(BF16)
        a_pre = _dot(ba, a2w_ref[...]) + a2b_ref[...]
        log_a = _log_sigmoid(a_pre) * (1.0 / GLA_GATE_NORMALIZER)
        logaf_ref[0, rows, :] = log_a[:, :B_KW]
        logab_ref[0, rows, :] = log_a[:, B_KW:]


def _even_in(x, wqk, wrest, convw, gateb, a2w, a2b):
    bsz, n, _ = x.shape
    tm = min(PROJ_TM, n)
    nt = n // tm
    r = tm // HALO
    nblk8 = n // HALO

    def tile(w, dt):
        return (pl.BlockSpec((1, tm, w), lambda b, i: (b, i, 0)), jax.ShapeDtypeStruct((bsz, n, w), dt))

    outs = [tile(3 * MIX_W, BF16), tile(MIX_W, BF16), tile(LANES, F32),
            tile(2 * B_KW + MIX_W, BF16), tile(MIX_W, BF16), tile(B_KW, F32), tile(B_KW, F32)]
    return pl.pallas_call(
        _even_in_kernel,
        grid=(bsz, nt),
        in_specs=[pl.BlockSpec((1, HALO, D_MODEL), lambda b, i: (b, jnp.maximum(i * r - 1, 0), 0)),
                  pl.BlockSpec((1, tm, D_MODEL), lambda b, i: (b, i, 0)),
                  pl.BlockSpec((1, HALO, D_MODEL), lambda b, i: (b, jnp.minimum((i + 1) * r, nblk8 - 1), 0)),
                  _full_spec(wqk.shape), _full_spec(wrest.shape), _full_spec(convw.shape),
                  _full_spec(gateb.shape), _full_spec(a2w.shape), _full_spec(a2b.shape)],
        out_specs=[o[0] for o in outs],
        out_shape=[o[1] for o in outs],
        scratch_shapes=[pltpu.VMEM((tm + 2 * HALO, 2 * A_W), F32)],
        compiler_params=_params(2),
        name="even_in",
    )(x, x, x, wqk, wrest, convw, gateb, a2w, a2b)


def _odd_in_kernel(x_ref, w_ref, lb_ref, cos_ref, sin_ref,
                   c_ref, lff_ref, lfb_ref, kcf_ref, kcb_ref, gc_ref, r_ref, gr_ref):
    qc_ref, vc_ref = _Cols(c_ref, 0, MIX_W), _Cols(c_ref, MIX_W, MIX_W)
    qr_ref, kr_ref, vr_ref = (_Cols(r_ref, j * MIX_W, MIX_W) for j in range(3))
    for r0 in range(0, x_ref.shape[1], PROJ_PIECE):
        rows = pl.ds(r0, PROJ_PIECE)
        xb = x_ref[0, rows, :].astype(BF16)

        def sect(idx):
            return _dot(xb, w_ref[:, idx * MIX_W:(idx + 1) * MIX_W])

        qc_ref[0, rows, :] = (_silu(sect(0)) * DH ** -0.5).astype(BF16)
        for d, (lf_ref, k_ref) in enumerate(((lff_ref, kcf_ref), (lfb_ref, kcb_ref))):
            z = sect(1 + d)
            lb = lb_ref[d:d + 1, :]
            f = lb + (1.0 - lb) * _sigmoid(z)
            lf_ref[0, rows, :] = jnp.log(jnp.maximum(f, TINY))
            k_ref[0, rows, :] = ((1.0 - lb) * _sigmoid(-z)).astype(BF16)
        vc_ref[0, rows, :] = sect(3).astype(BF16)
        gc_ref[0, rows, :] = _silu(sect(4)).astype(BF16)
        cos2 = cos_ref[rows, :]
        sin2 = sin_ref[rows, :]

        def rotary(y, scale):
            parts = []
            for h in range(HEADS):
                yh = y[:, h * DH:(h + 1) * DH]
                parts.append((yh * cos2 + pltpu.roll(yh, DH // 2, axis=1) * sin2) * scale)
            return jnp.concatenate(parts, axis=-1).astype(BF16)

        qr_ref[0, rows, :] = rotary(sect(5), 1.0)
        kr_ref[0, rows, :] = rotary(sect(6), DH ** -0.5)
        vr_ref[0, rows, :] = sect(7).astype(BF16)
        gr_ref[0, rows, :] = _silu(sect(8)).astype(BF16)


def _odd_in(x, w, lb, cos2, sin2):
    bsz, n, _ = x.shape
    tm = min(PROJ_TM, n)

    def tile(w, dt):
        return (pl.BlockSpec((1, tm, w), lambda b, i: (b, i, 0)), jax.ShapeDtypeStruct((bsz, n, w), dt))

    outs = [tile(2 * MIX_W, BF16), tile(MIX_W, F32), tile(MIX_W, F32), tile(MIX_W, BF16), tile(MIX_W, BF16),
            tile(MIX_W, BF16), tile(3 * MIX_W, BF16), tile(MIX_W, BF16)]
    return pl.pallas_call(
        _odd_in_kernel,
        grid=(bsz, n // tm),
        in_specs=[pl.BlockSpec((1, tm, D_MODEL), lambda b, i: (b, i, 0)),
                  _const_spec(w.shape), _const_spec(lb.shape),
                  pl.BlockSpec((tm, DH), lambda b, i: (i, 0)),
                  pl.BlockSpec((tm, DH), lambda b, i: (i, 0))],
        out_specs=[o[0] for o in outs],
        out_shape=[o[1] for o in outs],
        compiler_params=_params(2),
        name="odd_in",
    )(x, w, lb, cos2, sin2)


def _hgrn_lower_bound(lb_logits, layer_idx):
    n_odd = lb_logits.shape[0]

    def kern(l_ref, o_ref):
        rows = [l_ref[i] for i in range(n_odd)]
        mx = rows[0]
        for r in rows[1:]:
            mx = jnp.maximum(mx, r)
        es = [jnp.exp(r - mx) for r in rows]
        tot = es[0]
        for e in es[1:]:
            tot = tot + e
        acc = jnp.zeros_like(tot)
        for i in range(1, layer_idx + 1):
            acc = acc + es[i] / tot
        o_ref[...] = acc

    return pl.pallas_call(
        kern,
        out_shape=jax.ShapeDtypeStruct(lb_logits.shape[1:], F32),
        name="hgrn_lb",
    )(lb_logits)


def _emit_interleaved(stage_lists):
    for i in range(max(len(st) for st in stage_lists)):
        for st in stage_lists:
            if i < len(st):
                st[i]()


def _scan_specs(n, c):
    nc = n // c

    def fwd(w):
        return pl.BlockSpec((1, c, w), lambda b, i: (b, i, 0))

    def bwd(w):
        return pl.BlockSpec((1, c, w), lambda b, i: (b, nc - 1 - i, 0))

    return nc, fwd, bwd


def _mlstm_stages(cum_ref, sel_ref, data, c_ref, m_ref):
    c = data[0][0].shape[1]
    work = []

    def prep():
        row = lax.broadcasted_iota(jnp.int32, (c, c), 0)
        colm = lax.broadcasted_iota(jnp.int32, (c, c), 1)
        ones_blk = jnp.ones((c, DH), BF16)
        gate_lane = lax.broadcasted_iota(jnp.int32, (1, LANES), 1)
        for d, (q_ref, k_ref, v_ref, g_ref, o_ref) in enumerate(data):
            mask, end = (colm <= row, c - 1) if d == 0 else (colm >= row, 0)
            gates = g_ref[0] * LOG2E
            cum = _dot_exact(cum_ref[d], gates)
            comb = jnp.where(gate_lane < N_GATES, gates, cum)
            comb_hi = comb.astype(BF16)
            xs = jnp.concatenate([comb_hi, (comb - comb_hi.astype(F32)).astype(BF16)], axis=-1)
            gates_t = gates.T
            cum_t = cum.T
            for h in range(HEADS):
                ci, cf = 8 * d + h, 8 * d + 4 + h
                sl = slice(h * DH, (h + 1) * DH)
                s = d * HEADS + h
                q = q_ref[0, :, sl]
                k = k_ref[0, :, sl]
                v_ext = jnp.concatenate([v_ref[0, :, sl], ones_blk], axis=-1)
                rep = _dot(xs, sel_ref[s])
                c_old = c_ref[s]
                work.append(dict(
                    s=s, sl=sl, o_ref=o_ref, mask=mask, k=k, v_ext=v_ext, c_old=c_old,
                    i_rep=rep[:, :DH], b_rep=rep[:, DH:],
                    r_row=gates_t[ci:ci + 1, :] - cum_t[cf:cf + 1, :],
                    tot=rep[end:end + 1, DH:],
                    m_old=m_ref[s:s + 1, :],
                    scores=_dot_nt(q, k), inter=_dot(q, c_old.astype(BF16))))

    def stats():
        for w in work:
            w["dmat"] = jnp.where(w["mask"], _lane_tile(w["b_rep"], c) + w["r_row"], NEG_BIG)
            w["m_inter"] = w["b_rep"] + w["m_old"]
            w["m_t"] = jnp.maximum(w["m_inter"], jnp.max(w["dmat"], axis=-1, keepdims=True))

    def weights():
        for w in work:
            w["att"] = (w["scores"] * jnp.exp2(w["dmat"] - _lane_tile(w["m_t"], c))).astype(BF16)
            w["sc"] = jnp.exp2(w["m_inter"] - w["m_t"])

    def outputs():
        for w in work:
            ext = _lane_tile(w["sc"], 2 * DH) * w["inter"] + _dot(w["att"], w["v_ext"])
            num, den = ext[:, :DH], ext[:, DH:]
            w["o_ref"][0, :, w["sl"]] = (num / jnp.maximum(jnp.abs(den), jnp.exp2(-w["m_t"]))).astype(BF16)

    def update():
        for w in work:
            s = w["s"]
            g_end = w["tot"] + w["i_rep"] - w["b_rep"]
            m_new = jnp.maximum(w["tot"] + w["m_old"], jnp.max(g_end, axis=0, keepdims=True))
            decay = jnp.exp2(w["tot"] + w["m_old"] - m_new)
            wk = (w["k"].astype(F32) * jnp.exp2(g_end - m_new)).astype(BF16)
            c_ref[s] = _lane_tile(decay, 2 * DH) * w["c_old"] + _dot_tn(wk, w["v_ext"])
            m_ref[s:s + 1, :] = m_new

    return [prep, stats, weights, outputs, update]


def _mlstm_consts(c):
    low = np.tril(np.ones((c, c), np.float32))
    cum = np.stack([low, low.T])
    sel = np.zeros((2 * HEADS, 2 * LANES, 2 * DH), np.float32)
    for d in range(2):
        for h in range(HEADS):
            for part in range(2):
                sel[d * HEADS + h, part * LANES + 8 * d + h, :DH] = 1.0
                sel[d * HEADS + h, part * LANES + N_GATES + 8 * d + 4 + h, DH:] = 1.0
    return jnp.asarray(cum, BF16), jnp.asarray(sel, BF16)


def _vec_levels(c):
    lv, h = [], VEC_SUB
    while h < c:
        lv.append(h)
        h *= 2
    return lv


def _vec_consts(c, hpg):
    levels = _vec_levels(c)
    t = np.arange(c)
    low = np.tril(np.ones((c, c), np.float32))
    lev = np.full((c, c), -2, np.int32)
    same_sub = (t[:, None] // VEC_SUB) == (t[None, :] // VEC_SUB)
    lev[same_sub & (t[None, :] <= t[:, None])] = -1
    for li, h in enumerate(levels):
        blk = t // (2 * h)
        second = (t // h) % 2 == 1
        pair = (blk[:, None] == blk[None, :]) & second[:, None] & (~second)[None, :]
        lev[pair] = li
    wall = jnp.asarray(np.stack([low, low.T]), BF16)
    levid = np.stack([np.tile(lev, (1, hpg)), np.tile(lev[::-1, ::-1], (1, hpg))])
    kmask = np.zeros((hpg * c, VEC_SUB * LANES), np.float32)
    lane_head = np.arange(LANES) // (LANES // hpg)
    for hh in range(hpg):
        for j in range(VEC_SUB):
            rows = hh * c + t[t % VEC_SUB == j]
            kmask[np.ix_(rows, j * LANES + np.nonzero(lane_head == hh)[0])] = 1.0
    return wall, jnp.asarray(levid, jnp.int32), jnp.asarray(kmask, BF16), tuple(levels)


def _level_exponent(b, h, reverse):
    pieces = []
    for r0 in range(0, b.shape[0], 2 * h):
        ref = r0 + h if reverse else r0 + h - 1
        bm = b[ref:ref + 1, :]
        first, second = b[r0:r0 + h], b[r0 + h:r0 + 2 * h]
        pieces += [first - bm, bm - second] if reverse else [bm - first, second - bm]
    return jnp.concatenate(pieces, axis=0)


def _vec_stages(hpg, levels, wall_ref, lev_ref, kmask_ref, data, st_ref, row0):
    c = VEC_CHUNK
    ngrp = data[0][0].shape[2] // LANES
    sub = VEC_SUB
    work = []

    def hsel(x, hh):
        if hpg == 1:
            return x
        lane = lax.broadcasted_iota(jnp.int32, (1, LANES), 1)
        return jnp.where(lane // (LANES // hpg) == hh, x, 0.0)

    def heads_stacked(x):
        return jnp.concatenate([hsel(x, hh) for hh in range(hpg)], axis=0).astype(BF16)

    def prep():
        kmask = kmask_ref[...]
        for d, (q_ref, k_ref, v_ref, g_ref, o_ref) in enumerate(data):
            rev = d == 1
            rows = pl.ds(row0[d], c)
            end = 0 if rev else c - 1
            b_all = _dot_exact(wall_ref[d], g_ref[0, rows, :] * LOG2E, terms=2)
            for grp in range(ngrp):
                ls = slice(grp * LANES, (grp + 1) * LANES)
                q = q_ref[0, rows, ls].astype(F32)
                kb = k_ref[0, rows, ls]
                k = kb.astype(F32)
                b = b_all[:, ls]
                b3 = b.reshape(c // sub, sub, LANES)
                parts = []
                for j in range(sub):
                    bj = jnp.broadcast_to(b3[:, j:j + 1, :], b3.shape).reshape(c, LANES)
                    parts.append((q * jnp.exp2(jnp.minimum(b - bj, 0.0))).astype(BF16))
                zs = []
                for h in levels:
                    qk = jnp.concatenate(
                        [x[r0 + i * h:r0 + (i + 1) * h] for r0 in range(0, c, 2 * h)
                         for i, x in enumerate((q, k) if rev else (k, q))], axis=0)
                    zs.append(qk * jnp.exp2(_level_exponent(b, h, rev)))
                b_end = b[end:end + 1, :]
                work.append(dict(
                    d=d, rev=rev, grp=grp, s=d * ngrp + grp, rows=rows, v_ref=v_ref, o_ref=o_ref, zs=zs,
                    qe_cat=jnp.concatenate(parts, axis=-1),
                    kspread=jnp.tile(kb, (hpg, sub)) * kmask,
                    qe0=q * jnp.exp2(b), kw=k * jnp.exp2(b_end - b), decay=jnp.exp2(b_end)))

    def scores():
        for w in work:
            lev = lev_ref[w["d"]]
            a = jnp.where(lev == -1, _dot_nt(w["qe_cat"], w["kspread"]), 0.0)
            for li, (h, z) in enumerate(zip(levels, w["zs"])):
                q_rows = [(r0, r0 + h) if w["rev"] else (r0 + h, r0 + 2 * h) for r0 in range(0, c, 2 * h)]
                lhs = jnp.concatenate([z[lo:hi] for lo, hi in q_rows], axis=0).astype(BF16)
                sc = _dot_nt(lhs, heads_stacked(z))
                slabs = [a[r:r + h] for r in range(0, c, h)]
                for bi, (lo, hi) in enumerate(q_rows):
                    slabs[lo // h] = jnp.where(lev[lo:hi] == li, sc[bi * h:(bi + 1) * h], slabs[lo // h])
                a = jnp.concatenate(slabs, axis=0)
            w["a"] = a.astype(BF16)

    def outputs():
        for w in work:
            s = w["s"]
            st = st_ref[s]
            stb = st.astype(BF16)
            upd = None
            for hh in range(hpg):
                head = w["grp"] * hpg + hh
                vs = slice(head * DH, (head + 1) * DH)
                v = w["v_ref"][0, w["rows"], vs]
                w["o_ref"][0, w["rows"], vs] = (_dot_nt(hsel(w["qe0"], hh).astype(BF16), stb)
                                                + _dot(w["a"][:, hh * c:(hh + 1) * c], v)).astype(BF16)
                u = _dot_tn(v, hsel(w["kw"], hh).astype(BF16))
                upd = u if upd is None else upd + u
            st_ref[s] = st * w["decay"] + upd

    return [prep, scores, outputs]


def _vec_subchunks(hpg, levels, wall_ref, lev_ref, kmask_ref, data, st_ref):
    nsub = data[0][0].shape[1] // VEC_CHUNK
    return [_vec_stages(hpg, levels, wall_ref, lev_ref, kmask_ref, data, st_ref,
                        (t * VEC_CHUNK, (nsub - 1 - t) * VEC_CHUNK)) for t in range(nsub)]


def _ret_consts(c):
    log_gamma = np.log1p(-(2.0 ** (-5.0 - np.arange(HEADS, dtype=np.float64))))
    t = np.arange(c, dtype=np.float64)
    diff = t[:, None] - t[None, :]
    dmat = np.zeros((2, HEADS, c, c))
    qdec = np.zeros((2, c, MIX_W))
    kdec = np.zeros((2, c, MIX_W))
    for h in range(HEADS):
        lg = log_gamma[h]
        dmat[0, h] = np.where(diff >= 0, np.exp(lg * np.maximum(diff, 0)), 0.0)
        dmat[1, h] = dmat[0, h].T
        sl = slice(h * DH, (h + 1) * DH)
        qdec[0, :, sl] = np.exp(lg * (t + 1))[:, None]
        qdec[1, :, sl] = np.exp(lg * (c - t))[:, None]
        kdec[0, :, sl] = np.exp(lg * (c - 1 - t))[:, None]
        kdec[1, :, sl] = np.exp(lg * t)[:, None]
    sdec = tuple(float(np.exp(lg * c)) for lg in log_gamma)
    return (jnp.asarray(dmat, F32), jnp.asarray(qdec, F32), jnp.asarray(kdec, F32)), sdec


def _ret_stages(sdec, dmat_ref, qdec_ref, kdec_ref, data, st_ref):
    work = []

    def prep():
        for d, (q_ref, k_ref, v_ref, o_ref) in enumerate(data):
            for h in range(HEADS):
                sl = slice(h * DH, (h + 1) * DH)
                s = d * HEADS + h
                q = q_ref[0, :, sl]
                k = k_ref[0, :, sl]
                st = st_ref[s]
                qd = (q.astype(F32) * qdec_ref[d, :, sl]).astype(BF16)
                work.append(dict(d=d, h=h, s=s, sl=sl, o_ref=o_ref, v=v_ref[0, :, sl], st=st,
                                 scores=_dot_nt(q, k), inter=_dot(qd, st.astype(BF16)),
                                 kd=(k.astype(F32) * kdec_ref[d, :, sl]).astype(BF16)))

    def weights():
        for w in work:
            w["att"] = (w["scores"] * dmat_ref[w["d"], w["h"]]).astype(BF16)

    def outputs():
        for w in work:
            w["o_ref"][0, :, w["sl"]] = (w["inter"] + _dot(w["att"], w["v"])).astype(BF16)

    def update():
        for w in work:
            st_ref[w["s"]] = sdec[w["h"]] * w["st"] + _dot_tn(w["kd"], w["v"])

    return [prep, weights, outputs, update]


def _even_scan_kernel(hpg, levels, cum_ref, sel_ref, wall_ref, lev_ref, kmask_ref,
                      a_f, ag_f, a_b, ag_b, b_f, bg_f, b_b, bg_b, o_f, o_b, c_ref, m_ref, st_ref):
    @pl.when(pl.program_id(1) == 0)
    def _():
        c_ref[...] = jnp.zeros_like(c_ref)
        m_ref[...] = jnp.zeros_like(m_ref)
        st_ref[...] = jnp.zeros_like(st_ref)

    def mlstm_dir(a, g, o):
        return tuple(_Cols(a, j * MIX_W, MIX_W) for j in range(3)) + (g, _Cols(o, 0, MIX_W))

    def gla_dir(b, g, o):
        return (_Cols(b, 0, B_KW), _Cols(b, B_KW, B_KW), _Cols(b, 2 * B_KW, MIX_W), g, _Cols(o, MIX_W, MIX_W))

    mlstm = _mlstm_stages(cum_ref, sel_ref, (mlstm_dir(a_f, ag_f, o_f), mlstm_dir(a_b, ag_b, o_b)), c_ref, m_ref)
    gla = _vec_subchunks(hpg, levels, wall_ref, lev_ref, kmask_ref,
                         (gla_dir(b_f, bg_f, o_f), gla_dir(b_b, bg_b, o_b)), st_ref)
    _emit_interleaved([mlstm] + gla)


def _even_scan(a, gates, b, loga_f, loga_b):
    bsz, n, _ = a.shape
    c = min(SCALAR_CHUNK, n)
    nc, fwd, bwd = _scan_specs(n, c)
    hpg = DH // B_DK
    cum, sel = _mlstm_consts(c)
    wall, levid, kmask, levels = _vec_consts(VEC_CHUNK, hpg)
    consts = (cum, sel, wall, levid, kmask)
    widths = (a.shape[2], gates.shape[2])
    b_widths = (b.shape[2], loga_f.shape[2])
    out = jax.ShapeDtypeStruct((bsz, n, 2 * MIX_W), BF16)
    return pl.pallas_call(
        functools.partial(_even_scan_kernel, hpg, levels),
        grid=(bsz, nc),
        in_specs=([_const_spec(x.shape) for x in consts]
                  + [fwd(w) for w in widths] + [bwd(w) for w in widths]
                  + [fwd(w) for w in b_widths] + [bwd(w) for w in b_widths]),
        out_specs=[fwd(2 * MIX_W), bwd(2 * MIX_W)],
        out_shape=[out] * 2,
        scratch_shapes=[pltpu.VMEM((2 * HEADS, DH, 2 * DH), F32),
                        pltpu.VMEM((2 * HEADS, LANES), F32),
                        pltpu.VMEM((2 * B_KW // LANES, DH, LANES), F32)],
        compiler_params=_params(2),
        name="even_scan",
    )(*consts, a, gates, a, gates, b, loga_f, b, loga_b)


def _odd_scan_kernel(levels, sdec, wall_ref, lev_ref, kmask_ref, dmat_ref, qdec_ref, kdec_ref,
                     c_f, ck_f, cg_f, c_b, ck_b, cg_b, r_f, r_b, o_f, o_b, cst_ref, rst_ref):
    @pl.when(pl.program_id(1) == 0)
    def _():
        cst_ref[...] = jnp.zeros_like(cst_ref)
        rst_ref[...] = jnp.zeros_like(rst_ref)

    def hgrn_dir(cqv, k, g, o):
        return (_Cols(cqv, 0, MIX_W), k, _Cols(cqv, MIX_W, MIX_W), g, _Cols(o, 0, MIX_W))

    def ret_dir(r, o):
        return tuple(_Cols(r, j * MIX_W, MIX_W) for j in range(3)) + (_Cols(o, MIX_W, MIX_W),)

    hgrn = _vec_subchunks(1, levels, wall_ref, lev_ref, kmask_ref,
                          (hgrn_dir(c_f, ck_f, cg_f, o_f), hgrn_dir(c_b, ck_b, cg_b, o_b)), cst_ref)
    ret = _ret_stages(sdec, dmat_ref, qdec_ref, kdec_ref, (ret_dir(r_f, o_f), ret_dir(r_b, o_b)), rst_ref)
    _emit_interleaved(hgrn[:1] + [ret] + hgrn[1:])


def _odd_scan(cqv, kc_f, kc_b, lf_f, lf_b, r):
    bsz, n, _ = cqv.shape
    c = min(SCALAR_CHUNK, n)
    nc, fwd, bwd = _scan_specs(n, c)
    wall, levid, kmask, levels = _vec_consts(VEC_CHUNK, 1)
    ret_consts, sdec = _ret_consts(c)
    consts = (wall, levid, kmask) + ret_consts
    widths = (cqv.shape[2], MIX_W, MIX_W)
    out = jax.ShapeDtypeStruct((bsz, n, 2 * MIX_W), BF16)
    return pl.pallas_call(
        functools.partial(_odd_scan_kernel, levels, sdec),
        grid=(bsz, nc),
        in_specs=([_const_spec(x.shape) for x in consts]
                  + [fwd(w) for w in widths] + [bwd(w) for w in widths]
                  + [fwd(r.shape[2]), bwd(r.shape[2])]),
        out_specs=[fwd(2 * MIX_W), bwd(2 * MIX_W)],
        out_shape=[out] * 2,
        scratch_shapes=[pltpu.VMEM((2 * HEADS, DH, LANES), F32),
                        pltpu.VMEM((2 * HEADS, DH, DH), F32)],
        compiler_params=_params(2),
        name="odd_scan",
    )(*consts, cqv, kc_f, lf_f, cqv, kc_b, lf_b, r, r)


def _mix_ffn_kernel(norms, hf_ref, hb_ref, g1_ref, g2_ref, x_ref,
                    nw_ref, wmix_ref, lg1_ref, lb1_ref, win_ref, wout_ref, lg2_ref, lb2_ref, o_ref, x1_ref):
    s = pl.program_id(0)

    @pl.when(s == 0)
    def _():
        x1_ref[...] = jnp.zeros_like(x1_ref)

    cur = s % 2
    slab = x_ref.shape[0] // MIX_FFN_SLABS
    groups = ((g1_ref, norms[0], 0), (g2_ref, norms[1], MIX_W))

    vals = {}

    def gate_stage(i):
        def emit():
            rows = pl.ds(i * slab, slab)
            parts = []
            for gate_ref, norm, off in groups:
                hsum = (hf_ref[rows, off:off + MIX_W].astype(F32) + hb_ref[rows, off:off + MIX_W].astype(F32))
                gate = gate_ref[rows, :].astype(F32)
                for h in range(HEADS):
                    sl = slice(h * DH, (h + 1) * DH)
                    y = norm(hsum[:, sl]) * nw_ref[:, off + h * DH:off + (h + 1) * DH] * gate[:, sl]
                    parts.append(y.astype(BF16))
            vals[i] = jnp.concatenate(parts, axis=-1)
        return emit

    def proj_stage(i):
        def emit():
            vals[i] = _dot(vals[i], wmix_ref[...])
        return emit

    def norm_stage(i):
        def emit():
            rows = pl.ds(i * slab, slab)
            x1_ref[cur, rows, :] = _layer_norm(ALPHA * x_ref[rows, :] + vals[i], lg1_ref[...], lb1_ref[...])
        return emit

    slots = [[] for _ in range(2 * MIX_FFN_SLABS + 1)]
    for i in range(MIX_FFN_SLABS):
        slots[2 * i].append(gate_stage(i))
        slots[2 * i + 1].append(proj_stage(i))
        slots[2 * i + 2].append(norm_stage(i))
    between = [lambda fs=fs: [f() for f in fs] for fs in slots]
    o_ref[...] = _ffn_block(x1_ref[1 - cur], win_ref, wout_ref, lg2_ref[...], lb2_ref[...], between)


def _mix_ffn(norms, hf, hb, g1, g2, x2d, nw, w_mix, lg1, lb1, w_in, w_out, layer, lg2, lb2):
    t = x2d.shape[0]
    tm = min(MIX_FFN_TM, t)
    nt = t // tm

    def tile(w):
        return pl.BlockSpec((tm, w), lambda s: (jnp.minimum(s, nt - 1), 0))

    flat = [a.reshape(t, a.shape[-1]) for a in (hf, hb, g1, g2)]
    consts = (nw, w_mix, lg1, lb1, w_in, w_out, lg2, lb2)
    return pl.pallas_call(
        functools.partial(_mix_ffn_kernel, norms),
        grid=(nt + 1,),
        in_specs=([tile(a.shape[-1]) for a in flat] + [tile(D_MODEL)]
                  + [_layer_spec(a.shape, layer) if a.ndim == 3 else _const_spec(a.shape) for a in consts]),
        out_specs=pl.BlockSpec((tm, D_MODEL), lambda s: (jnp.maximum(s - 1, 0), 0)),
        out_shape=jax.ShapeDtypeStruct((t, D_MODEL), F32),
        scratch_shapes=[pltpu.VMEM((2, tm, D_MODEL), F32)],
        compiler_params=_params(1),
        name="mix_ffn",
    )(*flat, x2d, *consts)


def _pad_cols(w, width):
    return jnp.pad(w, ((0, 0), (0, width - w.shape[1])))


def _even_weights(w_in, gate_b, conv_w, a2_w, a2_b):
    sizes = (A_W, A_W, A_W, A_W, N_GATES, B_KW, B_KW, MIX_W, MIX_W, 2 * B_RANK)
    offs = np.concatenate([[0], np.cumsum(sizes)])
    aq, ak, av, ao, ag, bq, bk, bv, bg, ba = [w_in[:, offs[i]:offs[i + 1]] for i in range(len(sizes))]
    wqk = jnp.concatenate([aq, ak], axis=1).astype(BF16)
    ag2 = _pad_cols(jnp.concatenate([ag, ag], axis=1), LANES)
    wrest = jnp.concatenate([av, ao, bq, bk, bv, bg, ag2, _pad_cols(ba, LANES)], axis=1).astype(BF16)
    convw = jnp.pad(conv_w, ((0, SUBLANES - CONV_W), (0, 0)))
    gateb = _pad_cols(jnp.concatenate([gate_b, gate_b])[None, :], LANES)
    a2w = jnp.zeros((LANES, 2 * B_KW), F32)
    a2w = a2w.at[0:B_RANK, 0:B_KW].set(a2_w[0]).at[B_RANK:2 * B_RANK, B_KW:].set(a2_w[1]).astype(BF16)
    a2b = jnp.concatenate([a2_b[0], a2_b[1]])[None, :]
    return wqk, wrest, convw, gateb, a2w, a2b


def _rotary_tables(n):
    inv = 1.0 / (ROPE_BASE ** (jnp.arange(0, DH, 2, dtype=F32) / DH))
    ang = jnp.arange(n, dtype=F32)[:, None] * inv[None, :]
    cos, sin = jnp.cos(ang), jnp.sin(ang)
    return jnp.concatenate([cos, cos], -1), jnp.concatenate([-sin, sin], -1)


def _trunk(x, p):
    bsz, n, _ = x.shape
    t = bsz * n
    x2 = x.reshape(t, D_MODEL)
    cos2, sin2 = _rotary_tables(n)
    for l in range(DEPTH):
        j = l // 2
        x2 = _ffn_ln(x2, p["ffn1_in"], p["ffn1_out"], l, p["ln_g"][l, 0][None], p["ln_b"][l, 0][None])
        x3 = x2.reshape(bsz, n, D_MODEL)
        if l % 2 == 0:
            a, oga, gates, b, gb, logaf, logab = _even_in(x3, *p["even"][j])
            hf, hb = _even_scan(a, gates, b, logaf, logab)
            norms, g1, g2 = (_head_ln, _head_rms), oga, gb
            nw, w_out = p["ev_nw"][j], p["ev_out"][j]
        else:
            lb = _hgrn_lower_bound(p["od_lb_logits"], j)
            cqv, lff, lfb, kcf, kcb, gc, r, gr = _odd_in(x3, p["od_in"][j], lb, cos2, sin2)
            hf, hb = _odd_scan(cqv, kcf, kcb, lff, lfb, r)
            norms, g1, g2 = (_head_rms, _head_ln), gc, gr
            nw, w_out = p["od_nw"][j], p["od_out"][j]
        x2 = _mix_ffn(norms, hf, hb, g1, g2, x2, nw, w_out,
                      p["ln_g"][l, 1][None], p["ln_b"][l, 1][None],
                      p["ffn2_in"], p["ffn2_out"], l, p["ln_g"][l, 2][None], p["ln_b"][l, 2][None])
    return x2.reshape(bsz, n, D_MODEL)


def kernel(x_prompt, x_sample, ffn1_w_in, ffn1_w_out, ffn2_w_in, ffn2_w_out, ln_g, ln_b, ev_w_in, ev_gate_b, ev_conv_w, ev_gla_a2_w, ev_gla_a2_b, ev_norm_w, ev_w_out, od_w_in, od_lb_logits, od_norm_w, od_w_out):
    n_even, n_odd = ev_w_in.shape[0], od_w_in.shape[0]
    p = {
        "ffn1_in": ffn1_w_in.astype(BF16), "ffn1_out": ffn1_w_out.astype(BF16),
        "ffn2_in": ffn2_w_in.astype(BF16), "ffn2_out": ffn2_w_out.astype(BF16),
        "ln_g": ln_g, "ln_b": ln_b,
        "even": [_even_weights(ev_w_in[j], ev_gate_b[j], ev_conv_w[j], ev_gla_a2_w[j], ev_gla_a2_b[j])
                 for j in range(n_even)],
        "ev_nw": [ev_norm_w[j][None, :] for j in range(n_even)],
        "ev_out": ev_w_out.astype(BF16),
        "od_in": od_w_in.astype(BF16),
        "od_lb_logits": od_lb_logits,
        "od_nw": [jnp.concatenate([od_norm_w[j], jnp.ones((MIX_W,), F32)])[None, :] for j in range(n_odd)],
        "od_out": od_w_out.astype(BF16),
    }
    return (_trunk(x_prompt, p), _trunk(x_sample, p))
```
